```python
import jax, jax.numpy as jnp
from jax import lax
import numpy as np

D_MODEL = 1024
BATCH = 32
SEQ = 2048
DEPTH = 2

RET_HEADS = 4
RET_DK = 128
RET_DV = 256
RET_CHUNK = 128
ROPE_BASE = 10000.0
HGRN_HEADS = 4
HGRN_DK = 128
HGRN_DV = 128
HGRN_CHUNK = 32
D_FF = 4 * D_MODEL
EPS = 1e-6
MIN_F = 1e-30

RET_QK = RET_HEADS * RET_DK
RET_V = RET_HEADS * RET_DV
HGRN_K = HGRN_HEADS * HGRN_DK
HGRN_V = HGRN_HEADS * HGRN_DV
IN_WIDTHS = (RET_QK, RET_QK, RET_V, RET_V, HGRN_K, HGRN_K, HGRN_V, HGRN_V, D_MODEL, D_MODEL)
D_IN = sum(IN_WIDTHS)
IN_SPLITS = tuple(int(s) for s in np.cumsum(IN_WIDTHS)[:-1])

kernel_name = "retnet_hgrn2_gated_hybrid"


def rmsnorm(x, g):
    xf = x.astype(jnp.float32)
    y = xf * lax.rsqrt(jnp.mean(xf * xf, axis=-1, keepdims=True) + EPS)
    return (y * g.astype(jnp.float32)).astype(x.dtype)


def group_rmsnorm(x, g, n_heads):
    B, S, W = x.shape
    xf = x.astype(jnp.float32).reshape(B, S, n_heads, W // n_heads)
    y = xf * lax.rsqrt(jnp.mean(xf * xf, axis=-1, keepdims=True) + EPS)
    return (y.reshape(B, S, W) * g.astype(jnp.float32)).astype(x.dtype)


def to_chunks(t, chunk):
    B, S, H, d = t.shape
    return t.reshape(B, S // chunk, chunk, H, d).transpose(1, 0, 3, 2, 4)


def from_chunks(t):
    N, B, H, C, d = t.shape
    return t.transpose(1, 0, 3, 2, 4).reshape(B, N * C, H * d)


def rotate_every_two(t):
    t1 = t[..., 0::2]
    t2 = t[..., 1::2]
    return jnp.stack((-t2, t1), axis=-1).reshape(t.shape)


def retnet_rotation(t):
    S = t.shape[1]
    angle = 1.0 / (ROPE_BASE ** jnp.linspace(0.0, 1.0, RET_DK // 2, dtype=jnp.float32))
    angle = jnp.repeat(angle, 2)
    phase = jnp.arange(S, dtype=jnp.float32)[:, None] * angle[None, :]
    cos = jnp.cos(phase)[None, :, None, :]
    sin = jnp.sin(phase)[None, :, None, :]
    return t * cos + rotate_every_two(t) * sin


def chunkwise_retention(q, k, v):
    B = q.shape[0]
    C = RET_CHUNK
    log_gamma = jnp.log1p(-jnp.exp2(-5.0 - jnp.arange(RET_HEADS, dtype=jnp.float32)))
    idx = jnp.arange(C, dtype=jnp.float32)
    dist = idx[:, None] - idx[None, :]
    causal = dist >= 0
    intra_decay = jnp.where(causal[None], jnp.exp(log_gamma[:, None, None] * jnp.where(causal, dist, 0.0)[None]), 0.0)
    q_decay = jnp.exp(log_gamma[:, None] * (idx + 1.0)[None])[..., None]
    k_decay = jnp.exp(log_gamma[:, None] * (C - 1.0 - idx)[None])[..., None]
    chunk_decay = jnp.exp(log_gamma * C)[:, None, None]

    def step(state, xs):
        q_c, k_c, v_c = xs
        scores = jnp.einsum('bhtd,bhsd->bhts', q_c, k_c) * intra_decay
        o = jnp.einsum('bhts,bhsv->bhtv', scores, v_c) + jnp.einsum('bhtd,bhdv->bhtv', q_c * q_decay, state)
        state = chunk_decay * state + jnp.einsum('bhsd,bhsv->bhdv', k_c * k_decay, v_c)
        return state, o

    init = jnp.zeros((B, RET_HEADS, RET_DK, RET_DV), jnp.float32)
    _, o = lax.scan(step, init, (to_chunks(q, C), to_chunks(k, C), to_chunks(v, C)))
    return from_chunks(o)


def chunkwise_hgrn2(q, k, i, log_f):
    B = q.shape[0]
    C = HGRN_CHUNK
    causal = jnp.tril(jnp.ones((C, C), dtype=bool))[:, :, None]

    def step(state, xs):
        q_c, k_c, i_c, lf_c = xs
        b = jnp.cumsum(lf_c, axis=2)
        o_inter = jnp.einsum('bhtd,bhdv->bhtv', q_c * jnp.exp(b), state)
        pair = b[:, :, :, None, :] - b[:, :, None, :, :]
        pair = jnp.where(causal, jnp.exp(jnp.where(causal, pair, 0.0)), 0.0)
        attn = jnp.einsum('bhtd,bhsd,bhtsd->bhts', q_c, k_c, pair)
        o = o_inter + jnp.einsum('bhts,bhsv->bhtv', attn, i_c)
        b_last = b[:, :, -1:, :]
        state = jnp.exp(b_last[:, :, 0, :])[..., None] * state + jnp.einsum('bhsd,bhsv->bhdv', k_c * jnp.exp(b_last - b), i_c)
        return state, o

    init = jnp.zeros((B, HGRN_HEADS, HGRN_DK, HGRN_DV), jnp.float32)
    _, o = lax.scan(step, init, (to_chunks(q, C), to_chunks(k, C), to_chunks(i, C), to_chunks(log_f, C)))
    return from_chunks(o)


def hybrid_mixer(h, w_in, ret_gn_g, hgrn_norm_g, lb, w_br_ret, w_br_hgrn, b_merge, w_out):
    B, S, _ = h.shape
    dt = h.dtype
    proj = h @ w_in
    (r_q, r_k, r_v, r_g, g_q, g_f, g_i, g_g, m_ret, m_hg) = jnp.split(proj, IN_SPLITS, axis=-1)

    rq = retnet_rotation(r_q.astype(jnp.float32).reshape(B, S, RET_HEADS, RET_DK))
    rk = retnet_rotation(r_k.astype(jnp.float32).reshape(B, S, RET_HEADS, RET_DK)) * (RET_DK ** -0.5)
    rv = r_v.astype(jnp.float32).reshape(B, S, RET_HEADS, RET_DV)
    o_ret = chunkwise_retention(rq, rk, rv).astype(dt)
    y_ret = jax.nn.silu(r_g) * group_rmsnorm(o_ret, ret_gn_g, RET_HEADS)

    fz = g_f.astype(jnp.float32)
    lbf = lb.astype(jnp.float32)
    f = lbf + (1.0 - lbf) * jax.nn.sigmoid(fz)
    log_f = jnp.log(jnp.maximum(f, MIN_F))
    hk = (1.0 - lbf) * jax.nn.sigmoid(-fz)
    hq = jax.nn.silu(g_q.astype(jnp.float32))
    hi = g_i.astype(jnp.float32)
    o_hg = chunkwise_hgrn2(hq.reshape(B, S, HGRN_HEADS, HGRN_DK), hk.reshape(B, S, HGRN_HEADS, HGRN_DK),
                           hi.reshape(B, S, HGRN_HEADS, HGRN_DV), log_f.reshape(B, S, HGRN_HEADS, HGRN_DK)).astype(dt)
    y_hg = rmsnorm(o_hg, hgrn_norm_g) * jax.nn.silu(g_g)

    merged = jax.nn.sigmoid(m_ret + b_merge[0]) * (y_ret @ w_br_ret) + jax.nn.sigmoid(m_hg + b_merge[1]) * (y_hg @ w_br_hgrn)
    return merged @ w_out


def squared_relu_mlp(h, w_up, w_down):
    a = jax.nn.relu(h @ w_up)
    return (a * a) @ w_down


def setup_inputs(seed: int = 0) -> dict:
    key = jax.random.key(seed)
    ks = jax.random.split(key, 16)
    f32 = jnp.float32

    def w(k, shape, fan_in):
        return jax.random.normal(k, shape, f32) * (fan_in ** -0.5)

    def gain(k, shape):
        return 1.0 + 0.02 * jax.random.normal(k, shape, f32)

    return {
        "x": jax.random.normal(ks[0], (BATCH, SEQ, D_MODEL), f32),
        "norm_mix_g": gain(ks[1], (DEPTH, D_MODEL)),
        "w_in": w(ks[2], (DEPTH, D_MODEL, D_IN), D_MODEL),
        "ret_gn_g": gain(ks[3], (DEPTH, RET_V)),
        "hgrn_norm_g": gain(ks[4], (DEPTH, HGRN_V)),
        "hgrn_lb_logits": 0.1 * jax.random.normal(ks[5], (DEPTH, HGRN_K), f32),
        "w_br_ret": w(ks[6], (DEPTH, RET_V, D_MODEL), RET_V),
        "w_br_hgrn": w(ks[7], (DEPTH, HGRN_V, D_MODEL), HGRN_V),
        "b_merge": 0.02 * jax.random.normal(ks[8], (DEPTH, 2, D_MODEL), f32),
        "w_out": w(ks[9], (DEPTH, D_MODEL, D_MODEL), D_MODEL),
        "norm_ffn_g": gain(ks[10], (DEPTH, D_MODEL)),
        "w_ffn_up": w(ks[11], (DEPTH, D_MODEL, D_FF), D_MODEL),
        "w_ffn_down": w(ks[12], (DEPTH, D_FF, D_MODEL), D_FF),
        "final_norm_g": gain(ks[13], (D_MODEL,)),
    }


def reference(x, norm_mix_g, w_in, ret_gn_g, hgrn_norm_g, hgrn_lb_logits, w_br_ret, w_br_hgrn,
              b_merge, w_out, norm_ffn_g, w_ffn_up, w_ffn_down, final_norm_g):
    lb_sm = jax.nn.softmax(hgrn_lb_logits.astype(jnp.float32), axis=0)
    lower_bounds = jnp.cumsum(lb_sm, axis=0) - lb_sm[0]
    for l in range(DEPTH):
        h = rmsnorm(x, norm_mix_g[l])
        x = x + hybrid_mixer(h, w_in[l], ret_gn_g[l], hgrn_norm_g[l], lower_bounds[l],
                             w_br_ret[l], w_br_hgrn[l], b_merge[l], w_out[l])
        h = rmsnorm(x, norm_ffn_g[l])
        x = x + squared_relu_mlp(h, w_ffn_up[l], w_ffn_down[l])
    return rmsnorm(x, final_norm_g)
```

```python
import functools

import numpy as np
import jax
import jax.numpy as jnp
from jax import lax
from jax.experimental import pallas as pl
from jax.experimental.pallas import tpu as pltpu

F32 = jnp.float32
BF16 = jnp.bfloat16

D_MODEL = 1024
RET_HEADS = 4
RET_DK = 128
RET_DV = 256
HGRN_HEADS = 4
HGRN_DK = 128
HGRN_DV = 128
ROPE_BASE = 10000.0
D_FF = 4 * D_MODEL
EPS = 1e-6
MIN_F = 1e-30

RET_QK = RET_HEADS * RET_DK
RET_V = RET_HEADS * RET_DV
HGRN_K = HGRN_HEADS * HGRN_DK
HGRN_V = HGRN_HEADS * HGRN_DV
IN_WIDTHS = (RET_QK, RET_QK, RET_V, RET_V, HGRN_K, HGRN_K, HGRN_V, HGRN_V, D_MODEL, D_MODEL)
D_IN = sum(IN_WIDTHS)
_OFF = tuple(int(v) for v in np.concatenate([[0], np.cumsum(IN_WIDTHS)]))
OFF_RQ, OFF_RK, OFF_RV, OFF_RG, OFF_GQ, OFF_GF, OFF_GI, OFF_GG, OFF_MR, OFF_MH = _OFF[:10]

CHUNK = 128
N_LEVELS = 7
ROW_TILE = 512
VMEM_LIMIT = 56 * 1024 * 1024


def _rotation_tables(seq):
    angle = 1.0 / (ROPE_BASE ** np.linspace(0.0, 1.0, RET_DK // 2))
    phase = np.arange(seq, dtype=np.float64)[:, None] * angle[None, :]
    cos = np.concatenate([np.cos(phase), np.cos(phase)], axis=1)
    sin = np.concatenate([-np.sin(phase), np.sin(phase)], axis=1)
    return jnp.asarray(cos, F32), jnp.asarray(sin, F32)


def _retention_tables():
    c = CHUNK
    log_gamma = np.log1p(-np.exp2(-5.0 - np.arange(RET_HEADS, dtype=np.float64)))
    idx = np.arange(c, dtype=np.float64)
    dist = idx[:, None] - idx[None, :]
    causal = dist >= 0
    scale = RET_DK ** -0.5
    intra = np.where(causal[None], np.exp(log_gamma[:, None, None] * np.where(causal, dist, 0.0)[None]), 0.0)
    q_decay = np.exp(log_gamma[:, None] * (idx + 1.0)[None])[..., None] * np.ones((1, 1, RET_DK))
    k_decay = np.exp(log_gamma[:, None] * (c - 1.0 - idx)[None])[..., None] * np.ones((1, 1, RET_DK))
    chunk_decay = tuple(float(v) for v in np.exp(log_gamma * c))
    return (jnp.asarray(intra * scale, F32), jnp.asarray(q_decay, F32),
            jnp.asarray(k_decay * scale, F32), chunk_decay)


def _hgrn_tables():
    c = CHUNK
    mats = np.zeros((N_LEVELS + 2, c, c), np.float64)
    block = np.zeros((N_LEVELS, c, c), np.float64)
    t = np.arange(c)
    for lv in range(N_LEVELS):
        half = 1 << lv
        for tok in range(c):
            if (tok >> lv) & 1:
                mid = (tok >> lv) << lv
                mats[lv, tok, mid:tok + 1] = 1.0
            else:
                mid = ((tok >> lv) + 1) << lv
                mats[lv, tok, tok + 1:mid] = 1.0
        block[lv] = ((t[:, None] >> (lv + 1)) == (t[None, :] >> (lv + 1))).astype(np.float64)
        del half
    mats[N_LEVELS] = (t[None, :] <= t[:, None]).astype(np.float64)
    mats[N_LEVELS + 1] = (t[None, :] > t[:, None]).astype(np.float64)
    m2 = np.concatenate([mats, mats], axis=2)
    return jnp.asarray(m2, BF16), jnp.asarray(block, F32)


def _qk_column_permutation():
    per_head = np.concatenate([np.arange(0, RET_DK, 2), np.arange(1, RET_DK, 2)])
    cols = np.arange(D_IN)
    for base in (OFF_RQ, OFF_RK):
        for h in range(RET_HEADS):
            s = base + h * RET_DK
            cols[s:s + RET_DK] = s + per_head
    return cols


def _rms(xf, g):
    return xf * lax.rsqrt(jnp.mean(xf * xf, axis=-1, keepdims=True) + EPS) * g


def _sigmoid(z):
    return 1.0 / (1.0 + jnp.exp(-z))


def _norm_inproj_kernel(x_ref, g_ref, w_ref, o_ref):
    h = _rms(x_ref[...], g_ref[...]).astype(BF16)
    step = 1024
    for j in range(D_IN // step):
        o_ref[:, j * step:(j + 1) * step] = jnp.dot(
            h, w_ref[:, j * step:(j + 1) * step], preferred_element_type=F32).astype(BF16)


def _norm_inproj(x2, g, w_in_bf16):
    rows = x2.shape[0]
    tm = min(ROW_TILE, rows)
    return pl.pallas_call(
        _norm_inproj_kernel,
        grid=(rows // tm,),
        in_specs=[
            pl.BlockSpec((tm, D_MODEL), lambda i: (i, 0)),
            pl.BlockSpec((1, D_MODEL), lambda i: (0, 0)),
            pl.BlockSpec((D_MODEL, D_IN), lambda i: (0, 0), pipeline_mode=pl.Buffered(1)),
        ],
        out_specs=pl.BlockSpec((tm, D_IN), lambda i: (i, 0)),
        out_shape=jax.ShapeDtypeStruct((rows, D_IN), BF16),
        compiler_params=pltpu.CompilerParams(
            dimension_semantics=("parallel",), vmem_limit_bytes=VMEM_LIMIT),
        name="norm_inproj",
    )(x2, g, w_in_bf16)


def _dot_nt(a, b):
    return lax.dot_general(a, b, (((1,), (1,)), ((), ())), preferred_element_type=F32)


def _dot_tn(a, b):
    return lax.dot_general(a, b, (((0,), (0,)), ((), ())), preferred_element_type=F32)


def _mixer_kernel(proj_ref, x_ref, cos_ref, sin_ref, intra_ref, qdec_ref, kdec_ref, m2_ref, blk_ref,
                  gn_ref, hn_ref, lbl_ref, bm_ref, wbr_ref, wbh_ref, wo_ref,
                  o_ref, sret_ref, shg_ref, yret_ref, ohg_ref, *, layer, chunk_decay):
    @pl.when(pl.program_id(1) == 0)
    def _():
        sret_ref[...] = jnp.zeros_like(sret_ref)
        shg_ref[...] = jnp.zeros_like(shg_ref)

    cos = cos_ref[...]
    sin = sin_ref[...]

    for h in range(RET_HEADS):
        qr = proj_ref[:, OFF_RQ + h * RET_DK:OFF_RQ + (h + 1) * RET_DK].astype(F32)
        kr = proj_ref[:, OFF_RK + h * RET_DK:OFF_RK + (h + 1) * RET_DK].astype(F32)
        v = proj_ref[:, OFF_RV + h * RET_DV:OFF_RV + (h + 1) * RET_DV]
        q = qr * cos + pltpu.roll(qr, RET_DK // 2, 1) * sin
        k = kr * cos + pltpu.roll(kr, RET_DK // 2, 1) * sin
        scores = _dot_nt(q.astype(BF16), k.astype(BF16)) * intra_ref[h]
        state = sret_ref[h]
        o = (jnp.dot(scores.astype(BF16), v, preferred_element_type=F32)
             + jnp.dot((q * qdec_ref[h]).astype(BF16), state.astype(BF16), preferred_element_type=F32))
        sret_ref[h] = chunk_decay[h] * state + _dot_tn((k * kdec_ref[h]).astype(BF16), v)
        gain = gn_ref[:, h * RET_DV:(h + 1) * RET_DV]
        on = o * lax.rsqrt(jnp.mean(o * o, axis=-1, keepdims=True) + EPS) * gain
        rg = proj_ref[:, OFF_RG + h * RET_DV:OFF_RG + (h + 1) * RET_DV].astype(F32)
        yret_ref[:, h * RET_DV:(h + 1) * RET_DV] = (rg * _sigmoid(rg) * on).astype(BF16)

    lbl = lbl_ref[...]
    e = jnp.exp(lbl - jnp.max(lbl, axis=0, keepdims=True))
    sm = e / jnp.sum(e, axis=0, keepdims=True)
    lb = sm[0:1]
    for j in range(1, layer + 1):
        lb = lb + sm[j:j + 1]
    lb = lb - sm[0:1]

    fz = proj_ref[:, OFF_GF:OFF_GF + HGRN_K].astype(F32)
    f = lb + (1.0 - lb) * _sigmoid(fz)
    lf = jnp.log(jnp.maximum(f, MIN_F))
    hk = (1.0 - lb) * _sigmoid(-fz)
    gq = proj_ref[:, OFF_GQ:OFF_GQ + HGRN_K].astype(F32)
    hq = gq * _sigmoid(gq)
    lf_hi = lf.astype(BF16)
    lf_lo = (lf - lf_hi.astype(F32)).astype(BF16)
    lf2 = jnp.concatenate([lf_hi, lf_lo], axis=0)

    row = lax.broadcasted_iota(jnp.int32, (CHUNK, HGRN_K), 0)
    attn = [None] * HGRN_HEADS
    for lv in range(N_LEVELS):
        g = jnp.exp(jnp.dot(m2_ref[lv], lf2, preferred_element_type=F32))
        upper = (row & (1 << lv)) != 0
        qg = jnp.where(upper, hq * g, 0.0).astype(BF16)
        kg = jnp.where(upper, 0.0, hk * g).astype(BF16)
        for h in range(HGRN_HEADS):
            sl = slice(h * HGRN_DK, (h + 1) * HGRN_DK)
            a = _dot_nt(qg[:, sl], kg[:, sl]) * blk_ref[lv]
            attn[h] = a if attn[h] is None else attn[h] + a

    b = jnp.dot(m2_ref[N_LEVELS], lf2, preferred_element_type=F32)
    qe = (hq * jnp.exp(b)).astype(BF16)
    kd = (hk * jnp.exp(jnp.dot(m2_ref[N_LEVELS + 1], lf2, preferred_element_type=F32))).astype(BF16)
    qk = hq * hk
    for h in range(HGRN_HEADS):
        sl = slice(h * HGRN_DK, (h + 1) * HGRN_DK)
        iv = proj_ref[:, OFF_GI + h * HGRN_DV:OFF_GI + (h + 1) * HGRN_DV]
        state_t = shg_ref[h]
        diag = jnp.sum(qk[:, sl], axis=-1, keepdims=True)
        o = (_dot_nt(qe[:, sl], state_t.astype(BF16))
             + jnp.dot(attn[h].astype(BF16), iv, preferred_element_type=F32)
             + diag * iv.astype(F32))
        ohg_ref[:, h * HGRN_DV:(h + 1) * HGRN_DV] = o
        shg_ref[h] = state_t * jnp.exp(b[CHUNK - 1:CHUNK, sl]) + _dot_tn(iv, kd[:, sl])

    gg = proj_ref[:, OFF_GG:OFF_GG + HGRN_V].astype(F32)
    y_hg = (_rms(ohg_ref[...], hn_ref[...]) * (gg * _sigmoid(gg))).astype(BF16)

    m_ret = proj_ref[:, OFF_MR:OFF_MR + D_MODEL].astype(F32) + bm_ref[0:1]
    m_hg = proj_ref[:, OFF_MH:OFF_MH + D_MODEL].astype(F32) + bm_ref[1:2]
    merged = (_sigmoid(m_ret) * jnp.dot(yret_ref[...], wbr_ref[...], preferred_element_type=F32)
              + _sigmoid(m_hg) * jnp.dot(y_hg, wbh_ref[...], preferred_element_type=F32))
    o_ref[...] = x_ref[...] + jnp.dot(merged.astype(BF16), wo_ref[...], preferred_element_type=F32)


def _mixer(proj, x2, tables, gn, hn, lbl, bm, wbr, wbh, wo, *, layer, batch, seq):
    cos, sin, intra, qdec, kdec, chunk_decay, m2, blk = tables
    n_chunks = seq // CHUNK
    row_map = lambda b, n: (b * n_chunks + n, 0)
    const2 = lambda b, n: (0, 0)
    const3 = lambda b, n: (0, 0, 0)
    depth = lbl.shape[0]
    kern = functools.partial(_mixer_kernel, layer=layer, chunk_decay=chunk_decay)
    return pl.pallas_call(
        kern,
        grid=(batch, n_chunks),
        in_specs=[
            pl.BlockSpec((CHUNK, D_IN), row_map),
            pl.BlockSpec((CHUNK, D_MODEL), row_map),
            pl.BlockSpec((CHUNK, RET_DK), lambda b, n: (n, 0)),
            pl.BlockSpec((CHUNK, RET_DK), lambda b, n: (n, 0)),
            pl.BlockSpec((RET_HEADS, CHUNK, CHUNK), const3),
            pl.BlockSpec((RET_HEADS, CHUNK, RET_DK), const3),
            pl.BlockSpec((RET_HEADS, CHUNK, RET_DK), const3),
            pl.BlockSpec((N_LEVELS + 2, CHUNK, 2 * CHUNK), const3),
            pl.BlockSpec((N_LEVELS, CHUNK, CHUNK), const3),
            pl.BlockSpec((1, RET_V), const2),
            pl.BlockSpec((1, HGRN_V), const2),
            pl.BlockSpec((depth, HGRN_K), const2),
            pl.BlockSpec((2, D_MODEL), const2),
            pl.BlockSpec((RET_V, D_MODEL), const2),
            pl.BlockSpec((HGRN_V, D_MODEL), const2),
            pl.BlockSpec((D_MODEL, D_MODEL), const2),
        ],
        out_specs=pl.BlockSpec((CHUNK, D_MODEL), row_map),
        out_shape=jax.ShapeDtypeStruct(x2.shape, F32),
        scratch_shapes=[
            pltpu.VMEM((RET_HEADS, RET_DK, RET_DV), F32),
            pltpu.VMEM((HGRN_HEADS, HGRN_DV, HGRN_DK), F32),
            pltpu.VMEM((CHUNK, RET_V), BF16),
            pltpu.VMEM((CHUNK, HGRN_V), F32),
        ],
        compiler_params=pltpu.CompilerParams(
            dimension_semantics=("parallel", "arbitrary"), vmem_limit_bytes=VMEM_LIMIT),
        name="mixer",
    )(proj, x2, cos, sin, intra, qdec, kdec, m2, blk, gn, hn, lbl, bm, wbr, wbh, wo)


def _ffn_kernel(x_ref, g_ref, wu_ref, wd_ref, fg_ref, o_ref, *, final_norm):
    x = x_ref[...]
    h = _rms(x, g_ref[...]).astype(BF16)
    step = 1024
    acc = x
    for j in range(D_FF // step):
        a = jnp.maximum(jnp.dot(h, wu_ref[:, j * step:(j + 1) * step], preferred_element_type=F32), 0.0)
        acc = acc + jnp.dot((a * a).astype(BF16), wd_ref[j * step:(j + 1) * step, :],
                            preferred_element_type=F32)
    if final_norm:
        acc = _rms(acc, fg_ref[...])
    o_ref[...] = acc


def _ffn(x2, g, wu, wd, fg, *, final_norm):
    rows = x2.shape[0]
    tm = min(ROW_TILE, rows)
    return pl.pallas_call(
        functools.partial(_ffn_kernel, final_norm=final_norm),
        grid=(rows // tm,),
        in_specs=[
            pl.BlockSpec((tm, D_MODEL), lambda i: (i, 0)),
            pl.BlockSpec((1, D_MODEL), lambda i: (0, 0)),
            pl.BlockSpec((D_MODEL, D_FF), lambda i: (0, 0), pipeline_mode=pl.Buffered(1)),
            pl.BlockSpec((D_FF, D_MODEL), lambda i: (0, 0), pipeline_mode=pl.Buffered(1)),
            pl.BlockSpec((1, D_MODEL), lambda i: (0, 0)),
        ],
        out_specs=pl.BlockSpec((tm, D_MODEL), lambda i: (i, 0)),
        out_shape=jax.ShapeDtypeStruct(x2.shape, F32),
        compiler_params=pltpu.CompilerParams(
            dimension_semantics=("parallel",), vmem_limit_bytes=VMEM_LIMIT),
        name="ffn",
    )(x2, g, wu, wd, fg)


def kernel(x, norm_mix_g, w_in, ret_gn_g, hgrn_norm_g, hgrn_lb_logits, w_br_ret, w_br_hgrn, b_merge, w_out,
           norm_ffn_g, w_ffn_up, w_ffn_down, final_norm_g):
    batch, seq, d_model = x.shape
    depth = w_in.shape[0]
    assert d_model == D_MODEL and seq % CHUNK == 0 and w_in.shape[2] == D_IN
    assert (batch * seq) % min(ROW_TILE, batch * seq) == 0

    cos, sin = _rotation_tables(seq)
    intra, qdec, kdec, chunk_decay = _retention_tables()
    m2, blk = _hgrn_tables()
    tables = (cos, sin, intra, qdec, kdec, chunk_decay, m2, blk)
    perm = _qk_column_permutation()

    x2 = x.reshape(batch * seq, d_model)
    lbl = hgrn_lb_logits.astype(F32)
    fg = final_norm_g.reshape(1, d_model)
    for l in range(depth):
        w_in_l = w_in[l][:, perm].astype(BF16)
        proj = _norm_inproj(x2, norm_mix_g[l].reshape(1, d_model), w_in_l)
        x2 = _mixer(proj, x2, tables,
                    ret_gn_g[l].reshape(1, RET_V), hgrn_norm_g[l].reshape(1, HGRN_V), lbl, b_merge[l],
                    w_br_ret[l].astype(BF16), w_br_hgrn[l].astype(BF16), w_out[l].astype(BF16),
                    layer=l, batch=batch, seq=seq)
        x2 = _ffn(x2, norm_ffn_g[l].reshape(1, d_model), w_ffn_up[l].astype(BF16),
                  w_ffn_down[l].astype(BF16), fg, final_norm=(l == depth - 1))
    return x2.reshape(batch, seq, d_model)
```

```python
import functools

import numpy as np
import jax
import jax.numpy as jnp
from jax import lax
from jax.experimental import pallas as pl
from jax.experimental.pallas import tpu as pltpu

F32 = jnp.float32
BF16 = jnp.bfloat16

D_MODEL = 1024
RET_HEADS = 4
RET_DK = 128
RET_DV = 256
HGRN_HEADS = 4
HGRN_DK = 128
HGRN_DV = 128
ROPE_BASE = 10000.0
D_FF = 4 * D_MODEL
EPS = 1e-6
MIN_F = 1e-30
NEG_LOG2E = -1.4426950408889634

RET_QK = RET_HEADS * RET_DK
RET_V = RET_HEADS * RET_DV
HGRN_K = HGRN_HEADS * HGRN_DK
HGRN_V = HGRN_HEADS * HGRN_DV
IN_WIDTHS = (RET_QK, RET_QK, RET_V, RET_V, HGRN_K, HGRN_K, HGRN_V, HGRN_V, D_MODEL, D_MODEL)
D_IN = sum(IN_WIDTHS)
_OFF = tuple(int(v) for v in np.concatenate([[0], np.cumsum(IN_WIDTHS)]))
OFF_RQ, OFF_RK, OFF_RV, OFF_RG, OFF_GQ, OFF_GF, OFF_GI, OFF_GG, OFF_MR, OFF_MH = _OFF[:10]

CHUNK = 128
N_LEVELS = 7
SUBLANES = 8
ROW_TILE = 512
GROUP = 4
VMEM_LIMIT = 56 * 1024 * 1024


def _rotation_tables(seq):
    angle = 1.0 / (ROPE_BASE ** np.linspace(0.0, 1.0, RET_DK // 2))
    phase = np.arange(seq, dtype=np.float64)[:, None] * angle[None, :]
    cos = np.concatenate([np.cos(phase), np.cos(phase)], axis=1)
    sin = np.concatenate([-np.sin(phase), np.sin(phase)], axis=1)
    return jnp.asarray(cos, F32), jnp.asarray(sin, F32)


def _retention_tables():
    c = CHUNK
    log_gamma = np.log1p(-np.exp2(-5.0 - np.arange(RET_HEADS, dtype=np.float64)))
    idx = np.arange(c, dtype=np.float64)
    dist = idx[:, None] - idx[None, :]
    causal = dist >= 0
    scale = RET_DK ** -0.5
    intra = np.where(causal[None], np.exp(log_gamma[:, None, None] * np.where(causal, dist, 0.0)[None]), 0.0)
    q_decay = np.exp(log_gamma[:, None] * (idx + 1.0)[None])[..., None] * np.ones((1, 1, RET_DK))
    k_decay = np.exp(log_gamma[:, None] * (c - 1.0 - idx)[None])[..., None] * np.ones((1, 1, RET_DK))
    chunk_decay = tuple(float(v) for v in np.exp(log_gamma * c))
    return (jnp.asarray(intra * scale, F32), jnp.asarray(q_decay, F32),
            jnp.asarray(k_decay * scale, F32), chunk_decay)


def _hgrn_tables():
    t = np.arange(CHUNK)
    tri = (t[None, :] <= t[:, None]).astype(np.float64)
    m2 = np.concatenate([tri, tri], axis=1)
    diff = t[:, None] ^ t[None, :]
    level = np.where(t[:, None] > t[None, :], np.floor(np.log2(np.maximum(diff, 1))), -1.0)
    return jnp.asarray(m2, BF16), jnp.asarray(level, jnp.int32)


def _even_odd_heads(w_in_l):
    n_qk = 2 * RET_QK
    qk = w_in_l[:, :n_qk].reshape(D_MODEL, 2 * RET_HEADS, RET_DK // 2, 2)
    qk = qk.transpose(0, 1, 3, 2).reshape(D_MODEL, n_qk)
    return jnp.concatenate([qk, w_in_l[:, n_qk:]], axis=1)


def _rms(xf, g):
    return xf * lax.rsqrt(jnp.mean(xf * xf, axis=-1, keepdims=True) + EPS) * g


def _norm_inproj_kernel(x_ref, g_ref, w_ref, o_ref):
    h = _rms(x_ref[...], g_ref[...]).astype(BF16)
    step = 1024
    for j in range(D_IN // step):
        o_ref[:, j * step:(j + 1) * step] = jnp.dot(
            h, w_ref[:, j * step:(j + 1) * step], preferred_element_type=F32).astype(BF16)


def _norm_inproj(x2, g, w_in_bf16):
    rows = x2.shape[0]
    tm = min(ROW_TILE, rows)
    return pl.pallas_call(
        _norm_inproj_kernel,
        grid=(rows // tm,),
        in_specs=[
            pl.BlockSpec((tm, D_MODEL), lambda i: (i, 0)),
            pl.BlockSpec((1, D_MODEL), lambda i: (0, 0)),
            pl.BlockSpec((D_MODEL, D_IN), lambda i: (0, 0), pipeline_mode=pl.Buffered(1)),
        ],
        out_specs=pl.BlockSpec((tm, D_IN), lambda i: (i, 0)),
        out_shape=jax.ShapeDtypeStruct((rows, D_IN), BF16),
        compiler_params=pltpu.CompilerParams(
            dimension_semantics=("parallel",), vmem_limit_bytes=VMEM_LIMIT),
        name="norm_inproj",
    )(x2, g, w_in_bf16)


def _dot_nt(a, b):
    return lax.dot_general(a, b, (((1,), (1,)), ((), ())), preferred_element_type=F32)


def _dot_tn(a, b):
    return lax.dot_general(a, b, (((0,), (0,)), ((), ())), preferred_element_type=F32)


def _sigmoid_t(z):
    return 0.5 * jnp.tanh(0.5 * z) + 0.5


def _mixer_kernel(proj_ref, x_ref, cos_ref, sin_ref, intra_ref, qdec_ref, kdec_ref, m2_ref, lvl_ref,
                  gn_ref, hn_ref, lbl_ref, bm_ref, wbr_ref, wbh_ref, wo_ref,
                  o_ref, sret_ref, shg_ref, oret_ref, ohg_ref, qg_ref, kg_ref,
                  *, layer, chunk_decay, group):
    n = pl.program_id(1)

    @pl.when(n == 0)
    def _():
        sret_ref[...] = jnp.zeros_like(sret_ref)
        shg_ref[...] = jnp.zeros_like(shg_ref)

    rows = pl.ds(pl.multiple_of((n % group) * CHUNK, CHUNK), CHUNK)
    cos = cos_ref[...]
    sin = sin_ref[...]

    for h in range(RET_HEADS):
        qr = proj_ref[rows, OFF_RQ + h * RET_DK:OFF_RQ + (h + 1) * RET_DK].astype(F32)
        kr = proj_ref[rows, OFF_RK + h * RET_DK:OFF_RK + (h + 1) * RET_DK].astype(F32)
        v = proj_ref[rows, OFF_RV + h * RET_DV:OFF_RV + (h + 1) * RET_DV]
        q = qr * cos + pltpu.roll(qr, RET_DK // 2, 1) * sin
        k = kr * cos + pltpu.roll(kr, RET_DK // 2, 1) * sin
        scores = _dot_nt(q.astype(BF16), k.astype(BF16)) * intra_ref[h]
        state = sret_ref[h]
        o = (jnp.dot(scores.astype(BF16), v, preferred_element_type=F32)
             + jnp.dot((q * qdec_ref[h]).astype(BF16), state.astype(BF16), preferred_element_type=F32))
        sret_ref[h] = chunk_decay[h] * state + _dot_tn((k * kdec_ref[h]).astype(BF16), v)
        oret_ref[rows, h * RET_DV:(h + 1) * RET_DV] = (
            o * lax.rsqrt(jnp.mean(o * o, axis=-1, keepdims=True) + EPS))

    lbl = lbl_ref[...]
    e = jnp.exp(lbl - jnp.max(lbl, axis=0, keepdims=True))
    sm = e / jnp.sum(e, axis=0, keepdims=True)
    lb = sm[0:1]
    for j in range(1, layer + 1):
        lb = lb + sm[j:j + 1]
    lb = lb - sm[0:1]

    fz = proj_ref[rows, OFF_GF:OFF_GF + HGRN_K].astype(F32)
    sg = 1.0 / (1.0 + jnp.exp(-fz))
    lf = jnp.log(jnp.maximum(lb + (1.0 - lb) * sg, MIN_F))
    hk = (1.0 - lb) * (1.0 - sg)
    gq = proj_ref[rows, OFF_GQ:OFF_GQ + HGRN_K].astype(F32)
    hq = gq * _sigmoid_t(gq)
    lf_hi = lf.astype(BF16)
    lf_lo = (lf - lf_hi.astype(F32)).astype(BF16)
    lf2 = jnp.concatenate([lf_hi, lf_lo], axis=0)

    b = jnp.dot(m2_ref[...], lf2, preferred_element_type=F32)
    for lv in range(N_LEVELS):
        half = 1 << lv
        rows_per = max(2 * half, SUBLANES)
        b3 = b.reshape(CHUNK // rows_per, rows_per, HGRN_K)
        pos = lax.broadcasted_iota(jnp.int32, b3.shape, 1)
        ref = b3[:, half - 1:half, :]
        for blk in range(1, rows_per // (2 * half)):
            ref = jnp.where(pos >= blk * 2 * half, b3[:, blk * 2 * half + half - 1:blk * 2 * half + half, :], ref)
        g = jnp.exp2(jnp.abs(b3 - ref) * NEG_LOG2E).reshape(CHUNK, HGRN_K)
        qg_ref[lv] = (hq * g).astype(BF16)
        kg_ref[lv] = (hk * g).astype(BF16)
    qe = (hq * jnp.exp(b)).astype(BF16)
    kd = (hk * jnp.exp(b[CHUNK - 1:CHUNK] - b)).astype(BF16)
    qk = hq * hk
    lvl = lvl_ref[...]
    for h in range(HGRN_HEADS):
        sl = slice(h * HGRN_DK, (h + 1) * HGRN_DK)
        attn = jnp.zeros((CHUNK, CHUNK), F32)
        for lv in range(N_LEVELS):
            attn = jnp.where(lvl == lv, _dot_nt(qg_ref[lv, :, sl], kg_ref[lv, :, sl]), attn)
        iv = proj_ref[rows, OFF_GI + h * HGRN_DV:OFF_GI + (h + 1) * HGRN_DV]
        state_t = shg_ref[h]
        diag = jnp.sum(qk[:, sl], axis=-1, keepdims=True)
        ohg_ref[rows, h * HGRN_DV:(h + 1) * HGRN_DV] = (
            _dot_nt(qe[:, sl], state_t.astype(BF16))
            + jnp.dot(attn.astype(BF16), iv, preferred_element_type=F32)
            + diag * iv.astype(F32))
        shg_ref[h] = state_t * jnp.exp(b[CHUNK - 1:CHUNK, sl]) + _dot_tn(iv, kd[:, sl])

    @pl.when(n % group == group - 1)
    def _():
        rg = proj_ref[:, OFF_RG:OFF_RG + RET_V].astype(F32)
        y_ret = (rg * _sigmoid_t(rg) * (oret_ref[...] * gn_ref[...])).astype(BF16)
        gg = proj_ref[:, OFF_GG:OFF_GG + HGRN_V].astype(F32)
        y_hg = (_rms(ohg_ref[...], hn_ref[...]) * (gg * _sigmoid_t(gg))).astype(BF16)
        g_ret = _sigmoid_t(proj_ref[:, OFF_MR:OFF_MR + D_MODEL].astype(F32) + bm_ref[0:1])
        g_hg = _sigmoid_t(proj_ref[:, OFF_MH:OFF_MH + D_MODEL].astype(F32) + bm_ref[1:2])
        merged = (g_ret * jnp.dot(y_ret, wbr_ref[...], preferred_element_type=F32)
                  + g_hg * jnp.dot(y_hg, wbh_ref[...], preferred_element_type=F32))
        o_ref[...] = x_ref[...] + jnp.dot(merged.astype(BF16), wo_ref[...], preferred_element_type=F32)


def _mixer(proj, x2, tables, gn, hn, lbl, bm, wbr, wbh, wo, *, layer, batch, seq):
    cos, sin, intra, qdec, kdec, chunk_decay, m2, lvl = tables
    n_chunks = seq // CHUNK
    group = int(np.gcd(n_chunks, GROUP))
    n_groups = n_chunks // group
    tile = group * CHUNK
    group_map = lambda b, n: (b * n_groups + n // group, 0)
    const2 = lambda b, n: (0, 0)
    const3 = lambda b, n: (0, 0, 0)
    depth = lbl.shape[0]
    kern = functools.partial(_mixer_kernel, layer=layer, chunk_decay=chunk_decay, group=group)
    return pl.pallas_call(
        kern,
        grid=(batch, n_chunks),
        in_specs=[
            pl.BlockSpec((tile, D_IN), group_map),
            pl.BlockSpec((tile, D_MODEL), group_map),
            pl.BlockSpec((CHUNK, RET_DK), lambda b, n: (n, 0)),
            pl.BlockSpec((CHUNK, RET_DK), lambda b, n: (n, 0)),
            pl.BlockSpec((RET_HEADS, CHUNK, CHUNK), const3),
            pl.BlockSpec((RET_HEADS, CHUNK, RET_DK), const3),
            pl.BlockSpec((RET_HEADS, CHUNK, RET_DK), const3),
            pl.BlockSpec((CHUNK, 2 * CHUNK), const2),
            pl.BlockSpec((CHUNK, CHUNK), const2),
            pl.BlockSpec((1, RET_V), const2),
            pl.BlockSpec((1, HGRN_V), const2),
            pl.BlockSpec((depth, HGRN_K), const2),
            pl.BlockSpec((2, D_MODEL), const2),
            pl.BlockSpec((RET_V, D_MODEL), const2),
            pl.BlockSpec((HGRN_V, D_MODEL), const2),
            pl.BlockSpec((D_MODEL, D_MODEL), const2),
        ],
        out_specs=pl.BlockSpec((tile, D_MODEL), group_map),
        out_shape=jax.ShapeDtypeStruct(x2.shape, F32),
        scratch_shapes=[
            pltpu.VMEM((RET_HEADS, RET_DK, RET_DV), F32),
            pltpu.VMEM((HGRN_HEADS, HGRN_DV, HGRN_DK), F32),
            pltpu.VMEM((tile, RET_V), F32),
            pltpu.VMEM((tile, HGRN_V), F32),
            pltpu.VMEM((N_LEVELS, CHUNK, HGRN_K), BF16),
            pltpu.VMEM((N_LEVELS, CHUNK, HGRN_K), BF16),
        ],
        compiler_params=pltpu.CompilerParams(
            dimension_semantics=("parallel", "arbitrary"), vmem_limit_bytes=VMEM_LIMIT),
        name="mixer",
    )(proj, x2, cos, sin, intra, qdec, kdec, m2, lvl, gn, hn, lbl, bm, wbr, wbh, wo)


def _ffn_kernel(x_ref, g_ref, wu_ref, wd_ref, fg_ref, o_ref, *, final_norm):
    x = x_ref[...]
    h = _rms(x, g_ref[...]).astype(BF16)
    step = 1024
    acc = x
    for j in range(D_FF // step):
        a = jnp.maximum(jnp.dot(h, wu_ref[:, j * step:(j + 1) * step], preferred_element_type=F32), 0.0)
        acc = acc + jnp.dot((a * a).astype(BF16), wd_ref[j * step:(j + 1) * step, :],
                            preferred_element_type=F32)
    if final_norm:
        acc = _rms(acc, fg_ref[...])
    o_ref[...] = acc


def _ffn(x2, g, wu, wd, fg, *, final_norm):
    rows = x2.shape[0]
    tm = min(ROW_TILE, rows)
    return pl.pallas_call(
        functools.partial(_ffn_kernel, final_norm=final_norm),
        grid=(rows // tm,),
        in_specs=[
            pl.BlockSpec((tm, D_MODEL), lambda i: (i, 0)),
            pl.BlockSpec((1, D_MODEL), lambda i: (0, 0)),
            pl.BlockSpec((D_MODEL, D_FF), lambda i: (0, 0), pipeline_mode=pl.Buffered(1)),
            pl.BlockSpec((D_FF, D_MODEL), lambda i: (0, 0), pipeline_mode=pl.Buffered(1)),
            pl.BlockSpec((1, D_MODEL), lambda i: (0, 0)),
        ],
        out_specs=pl.BlockSpec((tm, D_MODEL), lambda i: (i, 0)),
        out_shape=jax.ShapeDtypeStruct(x2.shape, F32),
        compiler_params=pltpu.CompilerParams(
            dimension_semantics=("parallel",), vmem_limit_bytes=VMEM_LIMIT),
        name="ffn",
    )(x2, g, wu, wd, fg)


def kernel(x, norm_mix_g, w_in, ret_gn_g, hgrn_norm_g, hgrn_lb_logits, w_br_ret, w_br_hgrn, b_merge, w_out,
           norm_ffn_g, w_ffn_up, w_ffn_down, final_norm_g):
    batch, seq, d_model = x.shape
    depth = w_in.shape[0]
    assert d_model == D_MODEL and seq % CHUNK == 0 and w_in.shape[2] == D_IN
    assert (batch * seq) % min(ROW_TILE, batch * seq) == 0

    cos, sin = _rotation_tables(seq)
    intra, qdec, kdec, chunk_decay = _retention_tables()
    m2, lvl = _hgrn_tables()
    tables = (cos, sin, intra, qdec, kdec, chunk_decay, m2, lvl)

    x2 = x.reshape(batch * seq, d_model)
    lbl = hgrn_lb_logits.astype(F32)
    fg = final_norm_g.reshape(1, d_model)
    for l in range(depth):
        w_in_l = _even_odd_heads(w_in[l]).astype(BF16)
        proj = _norm_inproj(x2, norm_mix_g[l].reshape(1, d_model), w_in_l)
        x2 = _mixer(proj, x2, tables,
                    ret_gn_g[l].reshape(1, RET_V), hgrn_norm_g[l].reshape(1, HGRN_V), lbl, b_merge[l],
                    w_br_ret[l].astype(BF16), w_br_hgrn[l].astype(BF16), w_out[l].astype(BF16),
                    layer=l, batch=batch, seq=seq)
        x2 = _ffn(x2, norm_ffn_g[l].reshape(1, d_model), w_ffn_up[l].astype(BF16),
                  w_ffn_down[l].astype(BF16), fg, final_norm=(l == depth - 1))
    return x2.reshape(batch, seq, d_model)
```

```python
import functools

import numpy as np
import jax
import jax.numpy as jnp
from jax import lax
from jax.experimental import pallas as pl
from jax.experimental.pallas import tpu as pltpu

F32 = jnp.float32
BF16 = jnp.bfloat16

D_MODEL = 1024
RET_HEADS = 4
RET_DK = 128
RET_DV = 256
HGRN_HEADS = 4
HGRN_DK = 128
HGRN_DV = 128
ROPE_BASE = 10000.0
D_FF = 4 * D_MODEL
EPS = 1e-6
MIN_F = 1e-30
LOG2E = 1.4426950408889634

RET_QK = RET_HEADS * RET_DK
RET_V = RET_HEADS * RET_DV
HGRN_K = HGRN_HEADS * HGRN_DK
HGRN_V = HGRN_HEADS * HGRN_DV
IN_WIDTHS = (RET_QK, RET_QK, RET_V, RET_V, HGRN_K, HGRN_K, HGRN_V, HGRN_V, D_MODEL, D_MODEL)
D_IN = sum(IN_WIDTHS)
_OFF = tuple(int(v) for v in np.concatenate([[0], np.cumsum(IN_WIDTHS)]))
OFF_RQ, OFF_RK, OFF_RV, OFF_RG, OFF_GQ, OFF_GF, OFF_GI, OFF_GG, OFF_MR, OFF_MH = _OFF[:10]

CHUNK = 128
N_LEVELS = 7
SUBLANES = 8
ROW_TILE = 512
GROUP = 4
VMEM_LIMIT = 56 * 1024 * 1024


def _rotation_tables(seq):
    angle = 1.0 / (ROPE_BASE ** np.linspace(0.0, 1.0, RET_DK // 2))
    phase = np.arange(seq, dtype=np.float64)[:, None] * angle[None, :]
    cos = np.concatenate([np.cos(phase), np.cos(phase)], axis=1)
    sin = np.concatenate([-np.sin(phase), np.sin(phase)], axis=1)
    return jnp.asarray(cos, F32), jnp.asarray(sin, F32)


def _retention_tables():
    c = CHUNK
    log_gamma = np.log1p(-np.exp2(-5.0 - np.arange(RET_HEADS, dtype=np.float64)))
    idx = np.arange(c, dtype=np.float64)
    dist = idx[:, None] - idx[None, :]
    causal = dist >= 0
    scale = RET_DK ** -0.5
    intra = np.where(causal[None], np.exp(log_gamma[:, None, None] * np.where(causal, dist, 0.0)[None]), 0.0)
    q_decay = np.exp(log_gamma[:, None] * (idx + 1.0)[None])[..., None] * np.ones((1, 1, RET_DK))
    k_decay = np.exp(log_gamma[:, None] * (c - 1.0 - idx)[None])[..., None] * np.ones((1, 1, RET_DK))
    chunk_decay = tuple(float(v) for v in np.exp(log_gamma * c))
    return (jnp.asarray(intra * scale, F32), jnp.asarray(q_decay, F32),
            jnp.asarray(k_decay * scale, F32), chunk_decay)


def _hgrn_tables():
    t = np.arange(CHUNK)
    tri = (t[None, :] <= t[:, None]).astype(np.float64)
    m2 = np.concatenate([tri, tri], axis=1)
    diff = t[:, None] ^ t[None, :]
    level = np.where(t[:, None] > t[None, :], np.floor(np.log2(np.maximum(diff, 1))), -1.0)
    return jnp.asarray(m2, BF16), jnp.asarray(level, jnp.int32)


def _prepare_w_in(w_in_l):
    n_qk = 2 * RET_QK
    qk = w_in_l[:, :n_qk].reshape(D_MODEL, 2 * RET_HEADS, RET_DK // 2, 2)
    qk = qk.transpose(0, 1, 3, 2).reshape(D_MODEL, n_qk)
    scale = np.ones((D_IN,), np.float32)
    for off, width in ((OFF_RG, RET_V), (OFF_GQ, HGRN_K), (OFF_GG, HGRN_V), (OFF_MR, D_MODEL), (OFF_MH, D_MODEL)):
        scale[off:off + width] = 0.5
    return jnp.concatenate([qk, w_in_l[:, n_qk:]], axis=1) * scale


def _rms(xf, g):
    return xf * lax.rsqrt(jnp.mean(xf * xf, axis=-1, keepdims=True) + EPS) * g


def _norm_inproj_kernel(x_ref, g_ref, w_ref, o_ref):
    h = _rms(x_ref[...], g_ref[...]).astype(BF16)
    step = 1024
    for j in range(D_IN // step):
        o_ref[:, j * step:(j + 1) * step] = jnp.dot(
            h, w_ref[:, j * step:(j + 1) * step], preferred_element_type=F32).astype(BF16)


def _norm_inproj(x2, g, w_in_bf16):
    rows = x2.shape[0]
    tm = min(ROW_TILE, rows)
    return pl.pallas_call(
        _norm_inproj_kernel,
        grid=(rows // tm,),
        in_specs=[
            pl.BlockSpec((tm, D_MODEL), lambda i: (i, 0)),
            pl.BlockSpec((1, D_MODEL), lambda i: (0, 0)),
            pl.BlockSpec((D_MODEL, D_IN), lambda i: (0, 0), pipeline_mode=pl.Buffered(1)),
        ],
        out_specs=pl.BlockSpec((tm, D_IN), lambda i: (i, 0)),
        out_shape=jax.ShapeDtypeStruct((rows, D_IN), BF16),
        compiler_params=pltpu.CompilerParams(
            dimension_semantics=("parallel",), vmem_limit_bytes=VMEM_LIMIT),
        name="norm_inproj",
    )(x2, g, w_in_bf16)


def _dot_nt(a, b):
    return lax.dot_general(a, b, (((1,), (1,)), ((), ())), preferred_element_type=F32)


def _dot_tn(a, b):
    return lax.dot_general(a, b, (((0,), (0,)), ((), ())), preferred_element_type=F32)


def _silu_half(u):
    return u * (jnp.tanh(u) + 1.0)


def _mixer_kernel(proj_ref, x_ref, cos_ref, sin_ref, intra_ref, qdec_ref, kdec_ref, m2_ref, lvl_ref,
                  gn_ref, hn_ref, lbl_ref, bm_ref, wbr_ref, wbh_ref, wo_ref,
                  o_ref, sret_ref, shg_ref, oret_ref, ohg_ref, qg_ref, kg_ref,
                  *, layer, chunk_decay, group):
    @pl.when(pl.program_id(1) == 0)
    def _():
        sret_ref[...] = jnp.zeros_like(sret_ref)
        shg_ref[...] = jnp.zeros_like(shg_ref)

    lbl = lbl_ref[...]
    e = jnp.exp(lbl - jnp.max(lbl, axis=0, keepdims=True))
    sm = e / jnp.sum(e, axis=0, keepdims=True)
    lb = sm[0:1]
    for j in range(1, layer + 1):
        lb = lb + sm[j:j + 1]
    lb = lb - sm[0:1]
    lvl = lvl_ref[...]

    def chunk_body(sub):
        rows = slice(sub * CHUNK, (sub + 1) * CHUNK)
        cos = cos_ref[rows]
        sin = sin_ref[rows]

        for h in range(RET_HEADS):
            qr = proj_ref[rows, OFF_RQ + h * RET_DK:OFF_RQ + (h + 1) * RET_DK].astype(F32)
            kr = proj_ref[rows, OFF_RK + h * RET_DK:OFF_RK + (h + 1) * RET_DK].astype(F32)
            v = proj_ref[rows, OFF_RV + h * RET_DV:OFF_RV + (h + 1) * RET_DV]
            q = qr * cos + pltpu.roll(qr, RET_DK // 2, 1) * sin
            k = kr * cos + pltpu.roll(kr, RET_DK // 2, 1) * sin
            scores = _dot_nt(q.astype(BF16), k.astype(BF16)) * intra_ref[h]
            state = sret_ref[h]
            o = (jnp.dot(scores.astype(BF16), v, preferred_element_type=F32)
                 + jnp.dot((q * qdec_ref[h]).astype(BF16), state.astype(BF16), preferred_element_type=F32))
            sret_ref[h] = chunk_decay[h] * state + _dot_tn((k * kdec_ref[h]).astype(BF16), v)
            oret_ref[rows, h * RET_DV:(h + 1) * RET_DV] = (
                o * lax.rsqrt(jnp.mean(o * o, axis=-1, keepdims=True) + EPS))

        fz = proj_ref[rows, OFF_GF:OFF_GF + HGRN_K].astype(F32)
        sg = 1.0 / (1.0 + jnp.exp(-fz))
        lf = jnp.log(jnp.maximum(lb + (1.0 - lb) * sg, MIN_F))
        hk = (1.0 - lb) * (1.0 - sg)
        gq = proj_ref[rows, OFF_GQ:OFF_GQ + HGRN_K].astype(F32)
        hq = _silu_half(gq)
        lf_hi = lf.astype(BF16)
        lf_lo = (lf - lf_hi.astype(F32)).astype(BF16)
        lf2 = jnp.concatenate([lf_hi, lf_lo], axis=0)

        bs = jnp.dot(m2_ref[...], lf2, preferred_element_type=F32) * LOG2E
        hq16 = hq.astype(BF16)
        hk16 = hk.astype(BF16)
        for lv in range(N_LEVELS):
            half = 1 << lv
            rows_per = max(2 * half, SUBLANES)
            b3 = bs.reshape(CHUNK // rows_per, rows_per, HGRN_K)
            pos = lax.broadcasted_iota(jnp.int32, b3.shape, 1)
            ref = b3[:, half - 1:half, :]
            for blk in range(1, rows_per // (2 * half)):
                ref = jnp.where(pos >= blk * 2 * half,
                                b3[:, blk * 2 * half + half - 1:blk * 2 * half + half, :], ref)
            neg_abs = lax.bitcast_convert_type(
                lax.bitcast_convert_type(b3 - ref, jnp.uint32) | jnp.uint32(0x80000000), F32)
            g = jnp.exp2(neg_abs).reshape(CHUNK, HGRN_K).astype(BF16)
            qg_ref[lv] = hq16 * g
            kg_ref[lv] = hk16 * g
        qe = (hq * jnp.exp2(bs)).astype(BF16)
        kd = (hk * jnp.exp2(bs[CHUNK - 1:CHUNK] - bs)).astype(BF16)
        qk = hq * hk
        for h in range(HGRN_HEADS):
            sl = slice(h * HGRN_DK, (h + 1) * HGRN_DK)
            attn = jnp.zeros((CHUNK, CHUNK), F32)
            for lv in range(N_LEVELS):
                attn = jnp.where(lvl == lv, _dot_nt(qg_ref[lv, :, sl], kg_ref[lv, :, sl]), attn)
            iv = proj_ref[rows, OFF_GI + h * HGRN_DV:OFF_GI + (h + 1) * HGRN_DV]
            state_t = shg_ref[h]
            diag = jnp.sum(qk[:, sl], axis=-1, keepdims=True)
            ohg_ref[rows, h * HGRN_DV:(h + 1) * HGRN_DV] = (
                _dot_nt(qe[:, sl], state_t.astype(BF16))
                + jnp.dot(attn.astype(BF16), iv, preferred_element_type=F32)
                + diag * iv.astype(F32))
            shg_ref[h] = state_t * jnp.exp2(bs[CHUNK - 1:CHUNK, sl]) + _dot_tn(iv, kd[:, sl])

    for sub in range(group):
        chunk_body(sub)

    rg = proj_ref[:, OFF_RG:OFF_RG + RET_V].astype(F32)
    y_ret = (_silu_half(rg) * (oret_ref[...] * gn_ref[...])).astype(BF16)
    gg = proj_ref[:, OFF_GG:OFF_GG + HGRN_V].astype(F32)
    y_hg = (_rms(ohg_ref[...], hn_ref[...]) * _silu_half(gg)).astype(BF16)
    g_ret = jnp.tanh(proj_ref[:, OFF_MR:OFF_MR + D_MODEL].astype(F32) + bm_ref[0:1]) + 1.0
    g_hg = jnp.tanh(proj_ref[:, OFF_MH:OFF_MH + D_MODEL].astype(F32) + bm_ref[1:2]) + 1.0
    merged = (g_ret * jnp.dot(y_ret, wbr_ref[...], preferred_element_type=F32)
              + g_hg * jnp.dot(y_hg, wbh_ref[...], preferred_element_type=F32))
    o_ref[...] = x_ref[...] + jnp.dot(merged.astype(BF16), wo_ref[...], preferred_element_type=F32)


def _mixer(proj, x2, tables, gn, hn, lbl, bm_half, wbr, wbh, wo_half, *, layer, batch, seq):
    cos, sin, intra, qdec, kdec, chunk_decay, m2, lvl = tables
    n_chunks = seq // CHUNK
    group = int(np.gcd(n_chunks, GROUP))
    n_groups = n_chunks // group
    tile = group * CHUNK
    group_map = lambda b, n: (b * n_groups + n, 0)
    const2 = lambda b, n: (0, 0)
    const3 = lambda b, n: (0, 0, 0)
    depth = lbl.shape[0]
    kern = functools.partial(_mixer_kernel, layer=layer, chunk_decay=chunk_decay, group=group)
    return pl.pallas_call(
        kern,
        grid=(batch, n_groups),
        in_specs=[
            pl.BlockSpec((tile, D_IN), group_map),
            pl.BlockSpec((tile, D_MODEL), group_map),
            pl.BlockSpec((tile, RET_DK), lambda b, n: (n, 0)),
            pl.BlockSpec((tile, RET_DK), lambda b, n: (n, 0)),
            pl.BlockSpec((RET_HEADS, CHUNK, CHUNK), const3),
            pl.BlockSpec((RET_HEADS, CHUNK, RET_DK), const3),
            pl.BlockSpec((RET_HEADS, CHUNK, RET_DK), const3),
            pl.BlockSpec((CHUNK, 2 * CHUNK), const2),
            pl.BlockSpec((CHUNK, CHUNK), const2),
            pl.BlockSpec((1, RET_V), const2),
            pl.BlockSpec((1, HGRN_V), const2),
            pl.BlockSpec((depth, HGRN_K), const2),
            pl.BlockSpec((2, D_MODEL), const2),
            pl.BlockSpec((RET_V, D_MODEL), const2),
            pl.BlockSpec((HGRN_V, D_MODEL), const2),
            pl.BlockSpec((D_MODEL, D_MODEL), const2),
        ],
        out_specs=pl.BlockSpec((tile, D_MODEL), group_map),
        out_shape=jax.ShapeDtypeStruct(x2.shape, F32),
        scratch_shapes=[
            pltpu.VMEM((RET_HEADS, RET_DK, RET_DV), F32),
            pltpu.VMEM((HGRN_HEADS, HGRN_DV, HGRN_DK), F32),
            pltpu.VMEM((tile, RET_V), F32),
            pltpu.VMEM((tile, HGRN_V), F32),
            pltpu.VMEM((N_LEVELS, CHUNK, HGRN_K), BF16),
            pltpu.VMEM((N_LEVELS, CHUNK, HGRN_K), BF16),
        ],
        compiler_params=pltpu.CompilerParams(
            dimension_semantics=("parallel", "arbitrary"), vmem_limit_bytes=VMEM_LIMIT),
        name="mixer",
    )(proj, x2, cos, sin, intra, qdec, kdec, m2, lvl, gn, hn, lbl, bm_half, wbr, wbh, wo_half)


def _ffn_kernel(x_ref, g_ref, wu_ref, wd_ref, fg_ref, o_ref, *, final_norm):
    x = x_ref[...]
    h = _rms(x, g_ref[...]).astype(BF16)
    step = 1024
    acc = x
    for j in range(D_FF // step):
        a = jnp.maximum(jnp.dot(h, wu_ref[:, j * step:(j + 1) * step], preferred_element_type=F32), 0.0)
        acc = acc + jnp.dot((a * a).astype(BF16), wd_ref[j * step:(j + 1) * step, :],
                            preferred_element_type=F32)
    if final_norm:
        acc = _rms(acc, fg_ref[...])
    o_ref[...] = acc


def _ffn(x2, g, wu, wd, fg, *, final_norm):
    rows = x2.shape[0]
    tm = min(ROW_TILE, rows)
    return pl.pallas_call(
        functools.partial(_ffn_kernel, final_norm=final_norm),
        grid=(rows // tm,),
        in_specs=[
            pl.BlockSpec((tm, D_MODEL), lambda i: (i, 0)),
            pl.BlockSpec((1, D_MODEL), lambda i: (0, 0)),
            pl.BlockSpec((D_MODEL, D_FF), lambda i: (0, 0), pipeline_mode=pl.Buffered(1)),
            pl.BlockSpec((D_FF, D_MODEL), lambda i: (0, 0), pipeline_mode=pl.Buffered(1)),
            pl.BlockSpec((1, D_MODEL), lambda i: (0, 0)),
        ],
        out_specs=pl.BlockSpec((tm, D_MODEL), lambda i: (i, 0)),
        out_shape=jax.ShapeDtypeStruct(x2.shape, F32),
        compiler_params=pltpu.CompilerParams(
            dimension_semantics=("parallel",), vmem_limit_bytes=VMEM_LIMIT),
        name="ffn",
    )(x2, g, wu, wd, fg)


def kernel(x, norm_mix_g, w_in, ret_gn_g, hgrn_norm_g, hgrn_lb_logits, w_br_ret, w_br_hgrn, b_merge, w_out,
           norm_ffn_g, w_ffn_up, w_ffn_down, final_norm_g):
    batch, seq, d_model = x.shape
    depth = w_in.shape[0]
    assert d_model == D_MODEL and seq % CHUNK == 0 and w_in.shape[2] == D_IN
    assert (batch * seq) % min(ROW_TILE, batch * seq) == 0

    cos, sin = _rotation_tables(seq)
    intra, qdec, kdec, chunk_decay = _retention_tables()
    m2, lvl = _hgrn_tables()
    tables = (cos, sin, intra, qdec, kdec, chunk_decay, m2, lvl)

    x2 = x.reshape(batch * seq, d_model)
    lbl = hgrn_lb_logits.astype(F32)
    fg = final_norm_g.reshape(1, d_model)
    for l in range(depth):
        w_in_l = _prepare_w_in(w_in[l]).astype(BF16)
        proj = _norm_inproj(x2, norm_mix_g[l].reshape(1, d_model), w_in_l)
        x2 = _mixer(proj, x2, tables,
                    ret_gn_g[l].reshape(1, RET_V), hgrn_norm_g[l].reshape(1, HGRN_V), lbl, 0.5 * b_merge[l],
                    w_br_ret[l].astype(BF16), w_br_hgrn[l].astype(BF16), (0.5 * w_out[l]).astype(BF16),
                    layer=l, batch=batch, seq=seq)
        x2 = _ffn(x2, norm_ffn_g[l].reshape(1, d_model), w_ffn_up[l].astype(BF16),
                  w_ffn_down[l].astype(BF16), fg, final_norm=(l == depth - 1))
    return x2.reshape(batch, seq, d_model)
```

```python
import functools

import numpy as np
import jax
import jax.numpy as jnp
from jax import lax
from jax.experimental import pallas as pl
from jax.experimental.pallas import tpu as pltpu

F32 = jnp.float32
BF16 = jnp.bfloat16

D_MODEL = 1024
RET_HEADS = 4
RET_DK = 128
RET_DV = 256
HGRN_HEADS = 4
HGRN_DK = 128
HGRN_DV = 128
ROPE_BASE = 10000.0
D_FF = 4 * D_MODEL
EPS = 1e-6
MIN_F = 1e-30
LOG2E = 1.4426950408889634

RET_QK = RET_HEADS * RET_DK
RET_V = RET_HEADS * RET_DV
HGRN_K = HGRN_HEADS * HGRN_DK
HGRN_V = HGRN_HEADS * HGRN_DV
IN_WIDTHS = (RET_QK, RET_QK, RET_V, RET_V, HGRN_K, HGRN_K, HGRN_V, HGRN_V, D_MODEL, D_MODEL)
D_IN = sum(IN_WIDTHS)
_OFF = tuple(int(v) for v in np.concatenate([[0], np.cumsum(IN_WIDTHS)]))
OFF_RQ, OFF_RK, OFF_RV, OFF_RG, OFF_GQ, OFF_GF, OFF_GI, OFF_GG, OFF_MR, OFF_MH = _OFF[:10]

CHUNK = 128
N_LEVELS = 7
SUBLANES = 8
ROW_TILE = 512
GROUP = 4
VMEM_LIMIT = 56 * 1024 * 1024


def _rotation_tables(seq):
    angle = 1.0 / (ROPE_BASE ** jnp.linspace(0.0, 1.0, RET_DK // 2, dtype=F32))
    phase = jnp.arange(seq, dtype=F32)[:, None] * angle[None, :]
    cos = jnp.cos(phase)
    sin = jnp.sin(phase)
    return jnp.concatenate([cos, cos], axis=1), jnp.concatenate([-sin, sin], axis=1)


def _retention_tables():
    c = CHUNK
    log_gamma = np.log1p(-np.exp2(-5.0 - np.arange(RET_HEADS, dtype=np.float64)))
    idx = np.arange(c, dtype=np.float64)
    dist = idx[:, None] - idx[None, :]
    causal = dist >= 0
    scale = RET_DK ** -0.5
    intra = np.where(causal[None], np.exp(log_gamma[:, None, None] * np.where(causal, dist, 0.0)[None]), 0.0)
    q_decay = np.exp(log_gamma[:, None] * (idx + 1.0)[None])[..., None] * np.ones((1, 1, RET_DK))
    k_decay = np.exp(log_gamma[:, None] * (c - 1.0 - idx)[None])[..., None] * np.ones((1, 1, RET_DK))
    chunk_decay = tuple(float(v) for v in np.exp(log_gamma * c))
    return (jnp.asarray(intra * scale, F32), jnp.asarray(q_decay, F32),
            jnp.asarray(k_decay * scale, F32), chunk_decay)


def _hgrn_tables():
    t = np.arange(CHUNK)
    tri = (t[None, :] <= t[:, None]).astype(np.float64)
    m2 = np.concatenate([tri, tri], axis=1)
    diff = t[:, None] ^ t[None, :]
    level = np.where(t[:, None] > t[None, :], np.floor(np.log2(np.maximum(diff, 1))), -1.0)
    return jnp.asarray(m2, BF16), jnp.asarray(level, jnp.int32)


def _prepare_w_in(w_in):
    depth = w_in.shape[0]
    n_qk = 2 * RET_QK
    qk = w_in[:, :, :n_qk].reshape(depth, D_MODEL, 2 * RET_HEADS, RET_DK // 2, 2)
    qk = qk.transpose(0, 1, 2, 4, 3).reshape(depth, D_MODEL, n_qk)
    scale = np.ones((D_IN,), np.float32)
    for off, width in ((OFF_RG, RET_V), (OFF_GQ, HGRN_K), (OFF_GG, HGRN_V), (OFF_MR, D_MODEL), (OFF_MH, D_MODEL)):
        scale[off:off + width] = 0.5
    return (jnp.concatenate([qk, w_in[:, :, n_qk:]], axis=2) * scale).astype(BF16)


def _rms(xf, g):
    return xf * lax.rsqrt(jnp.mean(xf * xf, axis=-1, keepdims=True) + EPS) * g


def _norm_inproj_kernel(x_ref, g_ref, w_ref, o_ref):
    h = _rms(x_ref[...], g_ref[...]).astype(BF16)
    step = 1024
    for j in range(D_IN // step):
        o_ref[:, j * step:(j + 1) * step] = jnp.dot(
            h, w_ref[:, j * step:(j + 1) * step], preferred_element_type=F32).astype(BF16)


def _norm_inproj(x2, g_all, w_in_all, layer):
    rows = x2.shape[0]
    tm = min(ROW_TILE, rows)
    return pl.pallas_call(
        _norm_inproj_kernel,
        grid=(rows // tm,),
        in_specs=[
            pl.BlockSpec((tm, D_MODEL), lambda i: (i, 0)),
            pl.BlockSpec((None, 1, D_MODEL), lambda i: (layer, 0, 0)),
            pl.BlockSpec((None, D_MODEL, D_IN), lambda i: (layer, 0, 0), pipeline_mode=pl.Buffered(1)),
        ],
        out_specs=pl.BlockSpec((tm, D_IN), lambda i: (i, 0)),
        out_shape=jax.ShapeDtypeStruct((rows, D_IN), BF16),
        compiler_params=pltpu.CompilerParams(
            dimension_semantics=("parallel",), vmem_limit_bytes=VMEM_LIMIT),
        name="norm_inproj",
    )(x2, g_all, w_in_all)


def _dot_nt(a, b):
    return lax.dot_general(a, b, (((1,), (1,)), ((), ())), preferred_element_type=F32)


def _dot_tn(a, b):
    return lax.dot_general(a, b, (((0,), (0,)), ((), ())), preferred_element_type=F32)


def _silu_half(u):
    return u * (jnp.tanh(u) + 1.0)


def _mixer_kernel(proj_ref, x_ref, cos_ref, sin_ref, intra_ref, qdec_ref, kdec_ref, m2_ref, lvl_ref,
                  gn_ref, hn_ref, lbl_ref, bm_ref, wbr_ref, wbh_ref, wo_ref,
                  o_ref, sret_ref, shg_ref, oret_ref, ohg_ref, qg_ref, kg_ref, rq_ref, hx_ref,
                  *, layer, chunk_decay, group):
    @pl.when(pl.program_id(1) == 0)
    def _():
        sret_ref[...] = jnp.zeros_like(sret_ref)
        shg_ref[...] = jnp.zeros_like(shg_ref)

    lbl = lbl_ref[...]
    e = jnp.exp(lbl - jnp.max(lbl, axis=0, keepdims=True))
    sm = e / jnp.sum(e, axis=0, keepdims=True)
    lb = sm[0:1]
    for j in range(1, layer + 1):
        lb = lb + sm[j:j + 1]
    lb = lb - sm[0:1]
    lvl = lvl_ref[...]

    def prepare(sub):
        rows = slice(sub * CHUNK, (sub + 1) * CHUNK)
        slot = sub % 2
        cos = cos_ref[rows]
        sin = sin_ref[rows]
        for h in range(RET_HEADS):
            hs = slice(h * RET_DK, (h + 1) * RET_DK)
            qr = proj_ref[rows, OFF_RQ + h * RET_DK:OFF_RQ + (h + 1) * RET_DK].astype(F32)
            kr = proj_ref[rows, OFF_RK + h * RET_DK:OFF_RK + (h + 1) * RET_DK].astype(F32)
            q = qr * cos + pltpu.roll(qr, RET_DK // 2, 1) * sin
            k = kr * cos + pltpu.roll(kr, RET_DK // 2, 1) * sin
            rq_ref[slot, 0, :, hs] = q.astype(BF16)
            rq_ref[slot, 1, :, hs] = k.astype(BF16)
            rq_ref[slot, 2, :, hs] = (q * qdec_ref[h]).astype(BF16)
            rq_ref[slot, 3, :, hs] = (k * kdec_ref[h]).astype(BF16)

        fz = proj_ref[rows, OFF_GF:OFF_GF + HGRN_K].astype(F32)
        sg = 1.0 / (1.0 + jnp.exp(-fz))
        lf = jnp.log(jnp.maximum(lb + (1.0 - lb) * sg, MIN_F))
        hk = (1.0 - lb) * (1.0 - sg)
        gq = proj_ref[rows, OFF_GQ:OFF_GQ + HGRN_K].astype(F32)
        hq = _silu_half(gq)
        lf_hi = lf.astype(BF16)
        lf_lo = (lf - lf_hi.astype(F32)).astype(BF16)
        lf2 = jnp.concatenate([lf_hi, lf_lo], axis=0)

        bs = jnp.dot(m2_ref[...], lf2, preferred_element_type=F32) * LOG2E
        hq16 = hq.astype(BF16)
        hk16 = hk.astype(BF16)
        for lv in range(N_LEVELS):
            half = 1 << lv
            rows_per = max(2 * half, SUBLANES)
            b3 = bs.reshape(CHUNK // rows_per, rows_per, HGRN_K)
            pos = lax.broadcasted_iota(jnp.int32, b3.shape, 1)
            ref = b3[:, half - 1:half, :]
            for blk in range(1, rows_per // (2 * half)):
                ref = jnp.where(pos >= blk * 2 * half,
                                b3[:, blk * 2 * half + half - 1:blk * 2 * half + half, :], ref)
            neg_abs = lax.bitcast_convert_type(
                lax.bitcast_convert_type(b3 - ref, jnp.uint32) | jnp.uint32(0x80000000), F32)
            g = jnp.exp2(neg_abs).reshape(CHUNK, HGRN_K).astype(BF16)
            qg_ref[slot, lv] = hq16 * g
            kg_ref[slot, lv] = hk16 * g
        qg_ref[slot, N_LEVELS] = (hq * jnp.exp2(bs)).astype(BF16)
        kg_ref[slot, N_LEVELS] = (hk * jnp.exp2(bs[CHUNK - 1:CHUNK] - bs)).astype(BF16)
        hx_ref[slot, 0:CHUNK] = hq * hk
        hx_ref[slot, CHUNK:CHUNK + SUBLANES] = jnp.broadcast_to(jnp.exp2(bs[CHUNK - 1:CHUNK]), (SUBLANES, HGRN_K))

    def recur(sub):
        rows = slice(sub * CHUNK, (sub + 1) * CHUNK)
        slot = sub % 2
        for h in range(RET_HEADS):
            hs = slice(h * RET_DK, (h + 1) * RET_DK)
            v = proj_ref[rows, OFF_RV + h * RET_DV:OFF_RV + (h + 1) * RET_DV]
            scores = _dot_nt(rq_ref[slot, 0, :, hs], rq_ref[slot, 1, :, hs]) * intra_ref[h]
            state = sret_ref[h]
            o = (jnp.dot(scores.astype(BF16), v, preferred_element_type=F32)
                 + jnp.dot(rq_ref[slot, 2, :, hs], state.astype(BF16), preferred_element_type=F32))
            sret_ref[h] = chunk_decay[h] * state + _dot_tn(rq_ref[slot, 3, :, hs], v)
            oret_ref[rows, h * RET_DV:(h + 1) * RET_DV] = (
                o * lax.rsqrt(jnp.mean(o * o, axis=-1, keepdims=True) + EPS))

        for h in range(HGRN_HEADS):
            sl = slice(h * HGRN_DK, (h + 1) * HGRN_DK)
            attn = jnp.zeros((CHUNK, CHUNK), F32)
            for lv in range(N_LEVELS):
                attn = jnp.where(lvl == lv, _dot_nt(qg_ref[slot, lv, :, sl], kg_ref[slot, lv, :, sl]), attn)
            iv = proj_ref[rows, OFF_GI + h * HGRN_DV:OFF_GI + (h + 1) * HGRN_DV]
            state_t = shg_ref[h]
            diag = jnp.sum(hx_ref[slot, 0:CHUNK, sl], axis=-1, keepdims=True)
            ohg_ref[rows, h * HGRN_DV:(h + 1) * HGRN_DV] = (
                _dot_nt(qg_ref[slot, N_LEVELS, :, sl], state_t.astype(BF16))
                + jnp.dot(attn.astype(BF16), iv, preferred_element_type=F32)
                + diag * iv.astype(F32))
            shg_ref[h] = (state_t * hx_ref[slot, CHUNK:CHUNK + 1, sl]
                          + _dot_tn(iv, kg_ref[slot, N_LEVELS, :, sl]))

    prepare(0)
    for sub in range(group):
        if sub + 1 < group:
            prepare(sub + 1)
        recur(sub)

    rg = proj_ref[:, OFF_RG:OFF_RG + RET_V].astype(F32)
    y_ret = (_silu_half(rg) * (oret_ref[...] * gn_ref[...])).astype(BF16)
    gg = proj_ref[:, OFF_GG:OFF_GG + HGRN_V].astype(F32)
    y_hg = (_rms(ohg_ref[...], hn_ref[...]) * _silu_half(gg)).astype(BF16)
    g_ret = jnp.tanh(proj_ref[:, OFF_MR:OFF_MR + D_MODEL].astype(F32) + bm_ref[0:1]) + 1.0
    g_hg = jnp.tanh(proj_ref[:, OFF_MH:OFF_MH + D_MODEL].astype(F32) + bm_ref[1:2]) + 1.0
    merged = (g_ret * jnp.dot(y_ret, wbr_ref[...], preferred_element_type=F32)
              + g_hg * jnp.dot(y_hg, wbh_ref[...], preferred_element_type=F32))
    o_ref[...] = x_ref[...] + jnp.dot(merged.astype(BF16), wo_ref[...], preferred_element_type=F32)


def _mixer(proj, x2, tables, gn_all, hn_all, lbl, bm_half_all, wbr_all, wbh_all, wo_half_all, *, layer, batch, seq):
    cos, sin, intra, qdec, kdec, chunk_decay, m2, lvl = tables
    n_chunks = seq // CHUNK
    group = int(np.gcd(n_chunks, GROUP))
    n_groups = n_chunks // group
    tile = group * CHUNK
    group_map = lambda b, n: (b * n_groups + n, 0)
    const2 = lambda b, n: (0, 0)
    const3 = lambda b, n: (0, 0, 0)
    this_layer = lambda b, n: (layer, 0, 0)
    depth = lbl.shape[0]
    kern = functools.partial(_mixer_kernel, layer=layer, chunk_decay=chunk_decay, group=group)
    return pl.pallas_call(
        kern,
        grid=(batch, n_groups),
        in_specs=[
            pl.BlockSpec((tile, D_IN), group_map),
            pl.BlockSpec((tile, D_MODEL), group_map),
            pl.BlockSpec((tile, RET_DK), lambda b, n: (n, 0)),
            pl.BlockSpec((tile, RET_DK), lambda b, n: (n, 0)),
            pl.BlockSpec((RET_HEADS, CHUNK, CHUNK), const3),
            pl.BlockSpec((RET_HEADS, CHUNK, RET_DK), const3),
            pl.BlockSpec((RET_HEADS, CHUNK, RET_DK), const3),
            pl.BlockSpec((CHUNK, 2 * CHUNK), const2),
            pl.BlockSpec((CHUNK, CHUNK), const2),
            pl.BlockSpec((None, 1, RET_V), this_layer),
            pl.BlockSpec((None, 1, HGRN_V), this_layer),
            pl.BlockSpec((depth, HGRN_K), const2),
            pl.BlockSpec((None, 2, D_MODEL), this_layer),
            pl.BlockSpec((None, RET_V, D_MODEL), this_layer),
            pl.BlockSpec((None, HGRN_V, D_MODEL), this_layer),
            pl.BlockSpec((None, D_MODEL, D_MODEL), this_layer),
        ],
        out_specs=pl.BlockSpec((tile, D_MODEL), group_map),
        out_shape=jax.ShapeDtypeStruct(x2.shape, F32),
        scratch_shapes=[
            pltpu.VMEM((RET_HEADS, RET_DK, RET_DV), F32),
            pltpu.VMEM((HGRN_HEADS, HGRN_DV, HGRN_DK), F32),
            pltpu.VMEM((tile, RET_V), F32),
            pltpu.VMEM((tile, HGRN_V), F32),
            pltpu.VMEM((2, N_LEVELS + 1, CHUNK, HGRN_K), BF16),
            pltpu.VMEM((2, N_LEVELS + 1, CHUNK, HGRN_K), BF16),
            pltpu.VMEM((2, 4, CHUNK, RET_QK), BF16),
            pltpu.VMEM((2, CHUNK + SUBLANES, HGRN_K), F32),
        ],
        compiler_params=pltpu.CompilerParams(
            dimension_semantics=("parallel", "arbitrary"), vmem_limit_bytes=VMEM_LIMIT),
        name="mixer",
    )(proj, x2, cos, sin, intra, qdec, kdec, m2, lvl, gn_all, hn_all, lbl, bm_half_all, wbr_all, wbh_all, wo_half_all)


def _ffn_kernel(x_ref, g_ref, wu_ref, wd_ref, fg_ref, o_ref, *, final_norm):
    x = x_ref[...]
    h = _rms(x, g_ref[...]).astype(BF16)
    step = 1024
    acc = x
    for j in range(D_FF // step):
        a = jnp.maximum(jnp.dot(h, wu_ref[:, j * step:(j + 1) * step], preferred_element_type=F32), 0.0)
        acc = acc + jnp.dot((a * a).astype(BF16), wd_ref[j * step:(j + 1) * step, :],
                            preferred_element_type=F32)
    if final_norm:
        acc = _rms(acc, fg_ref[...])
    o_ref[...] = acc


def _ffn(x2, g_all, wu_all, wd_all, fg, *, layer, final_norm):
    rows = x2.shape[0]
    tm = min(ROW_TILE, rows)
    return pl.pallas_call(
        functools.partial(_ffn_kernel, final_norm=final_norm),
        grid=(rows // tm,),
        in_specs=[
            pl.BlockSpec((tm, D_MODEL), lambda i: (i, 0)),
            pl.BlockSpec((None, 1, D_MODEL), lambda i: (layer, 0, 0)),
            pl.BlockSpec((None, D_MODEL, D_FF), lambda i: (layer, 0, 0), pipeline_mode=pl.Buffered(1)),
            pl.BlockSpec((None, D_FF, D_MODEL), lambda i: (layer, 0, 0), pipeline_mode=pl.Buffered(1)),
            pl.BlockSpec((1, D_MODEL), lambda i: (0, 0)),
        ],
        out_specs=pl.BlockSpec((tm, D_MODEL), lambda i: (i, 0)),
        out_shape=jax.ShapeDtypeStruct(x2.shape, F32),
        compiler_params=pltpu.CompilerParams(
            dimension_semantics=("parallel",), vmem_limit_bytes=VMEM_LIMIT),
        name="ffn",
    )(x2, g_all, wu_all, wd_all, fg)


def kernel(x, norm_mix_g, w_in, ret_gn_g, hgrn_norm_g, hgrn_lb_logits, w_br_ret, w_br_hgrn, b_merge, w_out,
           norm_ffn_g, w_ffn_up, w_ffn_down, final_norm_g):
    batch, seq, d_model = x.shape
    depth = w_in.shape[0]
    assert d_model == D_MODEL and seq % CHUNK == 0 and w_in.shape[2] == D_IN
    assert (batch * seq) % min(ROW_TILE, batch * seq) == 0

    cos, sin = _rotation_tables(seq)
    intra, qdec, kdec, chunk_decay = _retention_tables()
    m2, lvl = _hgrn_tables()
    tables = (cos, sin, intra, qdec, kdec, chunk_decay, m2, lvl)

    w_in_all = _prepare_w_in(w_in)
    wbr_all = w_br_ret.astype(BF16)
    wbh_all = w_br_hgrn.astype(BF16)
    wo_half_all = (0.5 * w_out).astype(BF16)
    wu_all = w_ffn_up.astype(BF16)
    wd_all = w_ffn_down.astype(BF16)
    bm_half_all = 0.5 * b_merge
    g_mix = norm_mix_g.reshape(depth, 1, d_model)
    g_ffn = norm_ffn_g.reshape(depth, 1, d_model)
    gn_all = ret_gn_g.reshape(depth, 1, RET_V)
    hn_all = hgrn_norm_g.reshape(depth, 1, HGRN_V)

    x2 = x.reshape(batch * seq, d_model)
    lbl = hgrn_lb_logits.astype(F32)
    fg = final_norm_g.reshape(1, d_model)
    for l in range(depth):
        proj = _norm_inproj(x2, g_mix, w_in_all, l)
        x2 = _mixer(proj, x2, tables, gn_all, hn_all, lbl, bm_half_all, wbr_all, wbh_all, wo_half_all,
                    layer=l, batch=batch, seq=seq)
        x2 = _ffn(x2, g_ffn, wu_all, wd_all, fg, layer=l, final_norm=(l == depth - 1))
    return x2.reshape(batch, seq, d_model)
```

```python
import functools

import numpy as np
import jax
import jax.numpy as jnp
from jax import lax
from jax.experimental import pallas as pl
from jax.experimental.pallas import tpu as pltpu

F32 = jnp.float32
BF16 = jnp.bfloat16

D_MODEL = 1024
RET_HEADS = 4
RET_DK = 128
RET_DV = 256
HGRN_HEADS = 4
HGRN_DK = 128
HGRN_DV = 128
ROPE_BASE = 10000.0
D_FF = 4 * D_MODEL
EPS = 1e-6
MIN_F = 1e-30
LOG2E = 1.4426950408889634

RET_QK = RET_HEADS * RET_DK
RET_V = RET_HEADS * RET_DV
HGRN_K = HGRN_HEADS * HGRN_DK
HGRN_V = HGRN_HEADS * HGRN_DV
IN_WIDTHS = (RET_QK, RET_QK, RET_V, RET_V, HGRN_K, HGRN_K, HGRN_V, HGRN_V, D_MODEL, D_MODEL)
D_IN = sum(IN_WIDTHS)
_OFF = tuple(int(v) for v in np.concatenate([[0], np.cumsum(IN_WIDTHS)]))
OFF_RQ, OFF_RK, OFF_RV, OFF_RG, OFF_GQ, OFF_GF, OFF_GI, OFF_GG, OFF_MR, OFF_MH = _OFF[:10]

CHUNK = 128
N_LEVELS = 7
SUBLANES = 8
ROW_TILE = 512
GROUP = 4
VMEM_LIMIT = 56 * 1024 * 1024


def _rotation_tables(seq):
    angle = 1.0 / (ROPE_BASE ** jnp.linspace(0.0, 1.0, RET_DK // 2, dtype=F32))
    phase = jnp.arange(seq, dtype=F32)[:, None] * angle[None, :]
    cos = jnp.cos(phase)
    sin = jnp.sin(phase)
    return jnp.concatenate([cos, cos], axis=1), jnp.concatenate([-sin, sin], axis=1)


def _retention_tables():
    c = CHUNK
    log_gamma = np.log1p(-np.exp2(-5.0 - np.arange(RET_HEADS, dtype=np.float64)))
    idx = np.arange(c, dtype=np.float64)
    dist = idx[:, None] - idx[None, :]
    causal = dist >= 0
    scale = RET_DK ** -0.5
    intra = np.where(causal[None], np.exp(log_gamma[:, None, None] * np.where(causal, dist, 0.0)[None]), 0.0)
    q_decay = np.exp(log_gamma[:, None] * (idx + 1.0)[None])[..., None] * np.ones((1, 1, RET_DK))
    k_decay = np.exp(log_gamma[:, None] * (c - 1.0 - idx)[None])[..., None] * np.ones((1, 1, RET_DK))
    chunk_decay = tuple(float(v) for v in np.exp(log_gamma * c))
    return (jnp.asarray(intra * scale, F32), jnp.asarray(q_decay, F32),
            jnp.asarray(k_decay * scale, F32), chunk_decay)


def _hgrn_tables():
    t = np.arange(CHUNK)
    tri = (t[None, :] <= t[:, None]).astype(np.float64)
    m2 = np.concatenate([tri, tri], axis=1)
    diff = t[:, None] ^ t[None, :]
    level = np.where(t[:, None] > t[None, :], np.floor(np.log2(np.maximum(diff, 1))), -1.0)
    return jnp.asarray(m2, BF16), jnp.asarray(level, jnp.int32)


def _prepare_w_in(w_in):
    depth = w_in.shape[0]
    n_qk = 2 * RET_QK
    qk = w_in[:, :, :n_qk].reshape(depth, D_MODEL, 2 * RET_HEADS, RET_DK // 2, 2)
    qk = qk.transpose(0, 1, 2, 4, 3).reshape(depth, D_MODEL, n_qk)
    scale = np.ones((D_IN,), np.float32)
    for off, width in ((OFF_RG, RET_V), (OFF_GQ, HGRN_K), (OFF_GG, HGRN_V), (OFF_MR, D_MODEL), (OFF_MH, D_MODEL)):
        scale[off:off + width] = 0.5
    return (jnp.concatenate([qk, w_in[:, :, n_qk:]], axis=2) * scale).astype(BF16)


def _rms(xf, g):
    return xf * lax.rsqrt(jnp.mean(xf * xf, axis=-1, keepdims=True) + EPS) * g


def _norm_inproj_kernel(x_ref, g_ref, w_ref, o_ref):
    h = _rms(x_ref[...], g_ref[...]).astype(BF16)
    step = 1024
    for j in range(D_IN // step):
        o_ref[:, j * step:(j + 1) * step] = jnp.dot(
            h, w_ref[:, j * step:(j + 1) * step], preferred_element_type=F32).astype(BF16)


def _norm_inproj(x2, g_all, w_in_all, layer):
    rows = x2.shape[0]
    tm = min(ROW_TILE, rows)
    return pl.pallas_call(
        _norm_inproj_kernel,
        grid=(rows // tm,),
        in_specs=[
            pl.BlockSpec((tm, D_MODEL), lambda i: (i, 0)),
            pl.BlockSpec((None, 1, D_MODEL), lambda i: (layer, 0, 0)),
            pl.BlockSpec((None, D_MODEL, D_IN), lambda i: (layer, 0, 0), pipeline_mode=pl.Buffered(1)),
        ],
        out_specs=pl.BlockSpec((tm, D_IN), lambda i: (i, 0)),
        out_shape=jax.ShapeDtypeStruct((rows, D_IN), BF16),
        compiler_params=pltpu.CompilerParams(
            dimension_semantics=("parallel",), vmem_limit_bytes=VMEM_LIMIT),
        name="norm_inproj",
    )(x2, g_all, w_in_all)


def _dot_nt(a, b):
    return lax.dot_general(a, b, (((1,), (1,)), ((), ())), preferred_element_type=F32)


def _dot_tn(a, b):
    return lax.dot_general(a, b, (((0,), (0,)), ((), ())), preferred_element_type=F32)


def _silu_half(u):
    return u * (jnp.tanh(u) + 1.0)


def _mixer_kernel(proj_ref, x_ref, cos_ref, sin_ref, intra_ref, qdec_ref, kdec_ref, m2_ref, lvl_ref,
                  gn_ref, hn_ref, lbl_ref, bm_ref, wbr_ref, wbh_ref, wo_ref,
                  o_ref, sret_ref, shg_ref, oret_ref, ohg_ref, qg_ref, kgt_ref, kd_ref, rq_ref, rkt_ref, hx_ref,
                  *, layer, chunk_decay, group):
    @pl.when(pl.program_id(1) == 0)
    def _():
        sret_ref[...] = jnp.zeros_like(sret_ref)
        shg_ref[...] = jnp.zeros_like(shg_ref)

    lbl = lbl_ref[...]
    e = jnp.exp(lbl - jnp.max(lbl, axis=0, keepdims=True))
    sm = e / jnp.sum(e, axis=0, keepdims=True)
    lb = sm[0:1]
    for j in range(1, layer + 1):
        lb = lb + sm[j:j + 1]
    lb = lb - sm[0:1]
    lvl = lvl_ref[...]

    def prepare(sub):
        rows = slice(sub * CHUNK, (sub + 1) * CHUNK)
        slot = sub % 2
        cos = cos_ref[rows]
        sin = sin_ref[rows]
        for h in range(RET_HEADS):
            hs = slice(h * RET_DK, (h + 1) * RET_DK)
            qr = proj_ref[rows, OFF_RQ + h * RET_DK:OFF_RQ + (h + 1) * RET_DK].astype(F32)
            kr = proj_ref[rows, OFF_RK + h * RET_DK:OFF_RK + (h + 1) * RET_DK].astype(F32)
            q = qr * cos + pltpu.roll(qr, RET_DK // 2, 1) * sin
            k = kr * cos + pltpu.roll(kr, RET_DK // 2, 1) * sin
            rq_ref[slot, 0, :, hs] = q.astype(BF16)
            rkt_ref[slot, hs, :] = k.astype(BF16).T
            rq_ref[slot, 1, :, hs] = (q * qdec_ref[h]).astype(BF16)
            rq_ref[slot, 2, :, hs] = (k * kdec_ref[h]).astype(BF16)

        fz = proj_ref[rows, OFF_GF:OFF_GF + HGRN_K].astype(F32)
        sg = 1.0 / (1.0 + jnp.exp(-fz))
        lf = jnp.log(jnp.maximum(lb + (1.0 - lb) * sg, MIN_F))
        hk = (1.0 - lb) * (1.0 - sg)
        gq = proj_ref[rows, OFF_GQ:OFF_GQ + HGRN_K].astype(F32)
        hq = _silu_half(gq)
        lf_hi = lf.astype(BF16)
        lf_lo = (lf - lf_hi.astype(F32)).astype(BF16)
        lf2 = jnp.concatenate([lf_hi, lf_lo], axis=0)

        bs = jnp.dot(m2_ref[...], lf2, preferred_element_type=F32) * LOG2E
        hq16 = hq.astype(BF16)
        hk16 = hk.astype(BF16)
        for lv in range(N_LEVELS):
            half = 1 << lv
            rows_per = max(2 * half, SUBLANES)
            b3 = bs.reshape(CHUNK // rows_per, rows_per, HGRN_K)
            pos = lax.broadcasted_iota(jnp.int32, b3.shape, 1)
            ref = b3[:, half - 1:half, :]
            for blk in range(1, rows_per // (2 * half)):
                ref = jnp.where(pos >= blk * 2 * half,
                                b3[:, blk * 2 * half + half - 1:blk * 2 * half + half, :], ref)
            neg_abs = lax.bitcast_convert_type(
                lax.bitcast_convert_type(b3 - ref, jnp.uint32) | jnp.uint32(0x80000000), F32)
            g = jnp.exp2(neg_abs).reshape(CHUNK, HGRN_K).astype(BF16)
            qg_ref[slot, lv] = hq16 * g
            kgt_ref[slot, lv] = (hk16 * g).T
        qg_ref[slot, N_LEVELS] = (hq * jnp.exp2(bs)).astype(BF16)
        kd_ref[slot] = (hk * jnp.exp2(bs[CHUNK - 1:CHUNK] - bs)).astype(BF16)
        hx_ref[slot, 0:CHUNK] = hq * hk
        hx_ref[slot, CHUNK:CHUNK + SUBLANES] = jnp.broadcast_to(jnp.exp2(bs[CHUNK - 1:CHUNK]), (SUBLANES, HGRN_K))

    def recur(sub):
        rows = slice(sub * CHUNK, (sub + 1) * CHUNK)
        slot = sub % 2
        for h in range(RET_HEADS):
            hs = slice(h * RET_DK, (h + 1) * RET_DK)
            v = proj_ref[rows, OFF_RV + h * RET_DV:OFF_RV + (h + 1) * RET_DV]
            scores = jnp.dot(rq_ref[slot, 0, :, hs], rkt_ref[slot, hs, :], preferred_element_type=F32) * intra_ref[h]
            state = sret_ref[h]
            o = (jnp.dot(scores.astype(BF16), v, preferred_element_type=F32)
                 + jnp.dot(rq_ref[slot, 1, :, hs], state.astype(BF16), preferred_element_type=F32))
            sret_ref[h] = chunk_decay[h] * state + _dot_tn(rq_ref[slot, 2, :, hs], v)
            oret_ref[rows, h * RET_DV:(h + 1) * RET_DV] = (
                o * lax.rsqrt(jnp.mean(o * o, axis=-1, keepdims=True) + EPS))

        for h in range(HGRN_HEADS):
            sl = slice(h * HGRN_DK, (h + 1) * HGRN_DK)
            attn = jnp.zeros((CHUNK, CHUNK), F32)
            for lv in range(N_LEVELS):
                attn = jnp.where(lvl == lv, jnp.dot(qg_ref[slot, lv, :, sl], kgt_ref[slot, lv, sl, :],
                                                    preferred_element_type=F32), attn)
            iv = proj_ref[rows, OFF_GI + h * HGRN_DV:OFF_GI + (h + 1) * HGRN_DV]
            state_t = shg_ref[h]
            diag = jnp.sum(hx_ref[slot, 0:CHUNK, sl], axis=-1, keepdims=True)
            ohg_ref[rows, h * HGRN_DV:(h + 1) * HGRN_DV] = (
                _dot_nt(qg_ref[slot, N_LEVELS, :, sl], state_t.astype(BF16))
                + jnp.dot(attn.astype(BF16), iv, preferred_element_type=F32)
                + diag * iv.astype(F32))
            shg_ref[h] = (state_t * hx_ref[slot, CHUNK:CHUNK + 1, sl]
                          + _dot_tn(iv, kd_ref[slot, :, sl]))

    prepare(0)
    for sub in range(group):
        if sub + 1 < group:
            prepare(sub + 1)
        recur(sub)

    rg = proj_ref[:, OFF_RG:OFF_RG + RET_V].astype(F32)
    y_ret = (_silu_half(rg) * (oret_ref[...] * gn_ref[...])).astype(BF16)
    gg = proj_ref[:, OFF_GG:OFF_GG + HGRN_V].astype(F32)
    y_hg = (_rms(ohg_ref[...], hn_ref[...]) * _silu_half(gg)).astype(BF16)
    g_ret = jnp.tanh(proj_ref[:, OFF_MR:OFF_MR + D_MODEL].astype(F32) + bm_ref[0:1]) + 1.0
    g_hg = jnp.tanh(proj_ref[:, OFF_MH:OFF_MH + D_MODEL].astype(F32) + bm_ref[1:2]) + 1.0
    merged = (g_ret * jnp.dot(y_ret, wbr_ref[...], preferred_element_type=F32)
              + g_hg * jnp.dot(y_hg, wbh_ref[...], preferred_element_type=F32))
    o_ref[...] = x_ref[...] + jnp.dot(merged.astype(BF16), wo_ref[...], preferred_element_type=F32)


def _mixer(proj, x2, tables, gn_all, hn_all, lbl, bm_half_all, wbr_all, wbh_all, wo_half_all, *, layer, batch, seq):
    cos, sin, intra, qdec, kdec, chunk_decay, m2, lvl = tables
    n_chunks = seq // CHUNK
    group = int(np.gcd(n_chunks, GROUP))
    n_groups = n_chunks // group
    tile = group * CHUNK
    group_map = lambda b, n: (b * n_groups + n, 0)
    const2 = lambda b, n: (0, 0)
    const3 = lambda b, n: (0, 0, 0)
    this_layer = lambda b, n: (layer, 0, 0)
    depth = lbl.shape[0]
    kern = functools.partial(_mixer_kernel, layer=layer, chunk_decay=chunk_decay, group=group)
    return pl.pallas_call(
        kern,
        grid=(batch, n_groups),
        in_specs=[
            pl.BlockSpec((tile, D_IN), group_map),
            pl.BlockSpec((tile, D_MODEL), group_map),
            pl.BlockSpec((tile, RET_DK), lambda b, n: (n, 0)),
            pl.BlockSpec((tile, RET_DK), lambda b, n: (n, 0)),
            pl.BlockSpec((RET_HEADS, CHUNK, CHUNK), const3),
            pl.BlockSpec((RET_HEADS, CHUNK, RET_DK), const3),
            pl.BlockSpec((RET_HEADS, CHUNK, RET_DK), const3),
            pl.BlockSpec((CHUNK, 2 * CHUNK), const2),
            pl.BlockSpec((CHUNK, CHUNK), const2),
            pl.BlockSpec((None, 1, RET_V), this_layer),
            pl.BlockSpec((None, 1, HGRN_V), this_layer),
            pl.BlockSpec((depth, HGRN_K), const2),
            pl.BlockSpec((None, 2, D_MODEL), this_layer),
            pl.BlockSpec((None, RET_V, D_MODEL), this_layer),
            pl.BlockSpec((None, HGRN_V, D_MODEL), this_layer),
            pl.BlockSpec((None, D_MODEL, D_MODEL), this_layer),
        ],
        out_specs=pl.BlockSpec((tile, D_MODEL), group_map),
        out_shape=jax.ShapeDtypeStruct(x2.shape, F32),
        scratch_shapes=[
            pltpu.VMEM((RET_HEADS, RET_DK, RET_DV), F32),
            pltpu.VMEM((HGRN_HEADS, HGRN_DV, HGRN_DK), F32),
            pltpu.VMEM((tile, RET_V), F32),
            pltpu.VMEM((tile, HGRN_V), F32),
            pltpu.VMEM((2, N_LEVELS + 1, CHUNK, HGRN_K), BF16),
            pltpu.VMEM((2, N_LEVELS, HGRN_K, CHUNK), BF16),
            pltpu.VMEM((2, CHUNK, HGRN_K), BF16),
            pltpu.VMEM((2, 3, CHUNK, RET_QK), BF16),
            pltpu.VMEM((2, RET_QK, CHUNK), BF16),
            pltpu.VMEM((2, CHUNK + SUBLANES, HGRN_K), F32),
        ],
        compiler_params=pltpu.CompilerParams(
            dimension_semantics=("parallel", "arbitrary"), vmem_limit_bytes=VMEM_LIMIT),
        name="mixer",
    )(proj, x2, cos, sin, intra, qdec, kdec, m2, lvl, gn_all, hn_all, lbl, bm_half_all, wbr_all, wbh_all, wo_half_all)


def _ffn_kernel(x_ref, g_ref, wu_ref, wd_ref, fg_ref, o_ref, *, final_norm):
    x = x_ref[...]
    h = _rms(x, g_ref[...]).astype(BF16)
    step = 1024
    acc = x
    for j in range(D_FF // step):
        a = jnp.maximum(jnp.dot(h, wu_ref[:, j * step:(j + 1) * step], preferred_element_type=F32), 0.0)
        acc = acc + jnp.dot((a * a).astype(BF16), wd_ref[j * step:(j + 1) * step, :],
                            preferred_element_type=F32)
    if final_norm:
        acc = _rms(acc, fg_ref[...])
    o_ref[...] = acc


def _ffn(x2, g_all, wu_all, wd_all, fg, *, layer, final_norm):
    rows = x2.shape[0]
    tm = min(ROW_TILE, rows)
    return pl.pallas_call(
        functools.partial(_ffn_kernel, final_norm=final_norm),
        grid=(rows // tm,),
        in_specs=[
            pl.BlockSpec((tm, D_MODEL), lambda i: (i, 0)),
            pl.BlockSpec((None, 1, D_MODEL), lambda i: (layer, 0, 0)),
            pl.BlockSpec((None, D_MODEL, D_FF), lambda i: (layer, 0, 0), pipeline_mode=pl.Buffered(1)),
            pl.BlockSpec((None, D_FF, D_MODEL), lambda i: (layer, 0, 0), pipeline_mode=pl.Buffered(1)),
            pl.BlockSpec((1, D_MODEL), lambda i: (0, 0)),
        ],
        out_specs=pl.BlockSpec((tm, D_MODEL), lambda i: (i, 0)),
        out_shape=jax.ShapeDtypeStruct(x2.shape, F32),
        compiler_params=pltpu.CompilerParams(
            dimension_semantics=("parallel",), vmem_limit_bytes=VMEM_LIMIT),
        name="ffn",
    )(x2, g_all, wu_all, wd_all, fg)


def kernel(x, norm_mix_g, w_in, ret_gn_g, hgrn_norm_g, hgrn_lb_logits, w_br_ret, w_br_hgrn, b_merge, w_out,
           norm_ffn_g, w_ffn_up, w_ffn_down, final_norm_g):
    batch, seq, d_model = x.shape
    depth = w_in.shape[0]
    assert d_model == D_MODEL and seq % CHUNK == 0 and w_in.shape[2] == D_IN
    assert (batch * seq) % min(ROW_TILE, batch * seq) == 0

    cos, sin = _rotation_tables(seq)
    intra, qdec, kdec, chunk_decay = _retention_tables()
    m2, lvl = _hgrn_tables()
    tables = (cos, sin, intra, qdec, kdec, chunk_decay, m2, lvl)

    w_in_all = _prepare_w_in(w_in)
    wbr_all = w_br_ret.astype(BF16)
    wbh_all = w_br_hgrn.astype(BF16)
    wo_half_all = (0.5 * w_out).astype(BF16)
    wu_all = w_ffn_up.astype(BF16)
    wd_all = w_ffn_down.astype(BF16)
    bm_half_all = 0.5 * b_merge
    g_mix = norm_mix_g.reshape(depth, 1, d_model)
    g_ffn = norm_ffn_g.reshape(depth, 1, d_model)
    gn_all = ret_gn_g.reshape(depth, 1, RET_V)
    hn_all = hgrn_norm_g.reshape(depth, 1, HGRN_V)

    x2 = x.reshape(batch * seq, d_model)
    lbl = hgrn_lb_logits.astype(F32)
    fg = final_norm_g.reshape(1, d_model)
    for l in range(depth):
        proj = _norm_inproj(x2, g_mix, w_in_all, l)
        x2 = _mixer(proj, x2, tables, gn_all, hn_all, lbl, bm_half_all, wbr_all, wbh_all, wo_half_all,
                    layer=l, batch=batch, seq=seq)
        x2 = _ffn(x2, g_ffn, wu_all, wd_all, fg, layer=l, final_norm=(l == depth - 1))
    return x2.reshape(batch, seq, d_model)
```

```python
import functools

import numpy as np
import jax
import jax.numpy as jnp
from jax import lax
from jax.experimental import pallas as pl
from jax.experimental.pallas import tpu as pltpu

F32 = jnp.float32
BF16 = jnp.bfloat16

D_MODEL = 1024
RET_HEADS = 4
RET_DK = 128
RET_DV = 256
HGRN_HEADS = 4
HGRN_DK = 128
HGRN_DV = 128
ROPE_BASE = 10000.0
D_FF = 4 * D_MODEL
EPS = 1e-6
MIN_F = 1e-30
LOG2E = 1.4426950408889634

RET_QK = RET_HEADS * RET_DK
RET_V = RET_HEADS * RET_DV
HGRN_K = HGRN_HEADS * HGRN_DK
HGRN_V = HGRN_HEADS * HGRN_DV
IN_WIDTHS = (RET_QK, RET_QK, RET_V, RET_V, HGRN_K, HGRN_K, HGRN_V, HGRN_V, D_MODEL, D_MODEL)
D_IN = sum(IN_WIDTHS)
_OFF = tuple(int(v) for v in np.concatenate([[0], np.cumsum(IN_WIDTHS)]))
OFF_RQ, OFF_RK, OFF_RV, OFF_RG, OFF_GQ, OFF_GF, OFF_GI, OFF_GG, OFF_MR, OFF_MH = _OFF[:10]

CHUNK = 128
N_LEVELS = 7
SUBLANES = 8
ROW_TILE = 512
FFN_ROW_TILE = 1024
GROUP = 4
VMEM_LIMIT = 56 * 1024 * 1024


def _rotation_tables(seq):
    angle = 1.0 / (ROPE_BASE ** jnp.linspace(0.0, 1.0, RET_DK // 2, dtype=F32))
    phase = jnp.arange(seq, dtype=F32)[:, None] * angle[None, :]
    cos = jnp.cos(phase)
    sin = jnp.sin(phase)
    return jnp.concatenate([cos, cos], axis=1), jnp.concatenate([-sin, sin], axis=1)


def _retention_tables():
    c = CHUNK
    log_gamma = np.log1p(-np.exp2(-5.0 - np.arange(RET_HEADS, dtype=np.float64)))
    idx = np.arange(c, dtype=np.float64)
    dist = idx[:, None] - idx[None, :]
    causal = dist >= 0
    scale = RET_DK ** -0.5
    intra = np.where(causal[None], np.exp(log_gamma[:, None, None] * np.where(causal, dist, 0.0)[None]), 0.0)
    q_decay = np.exp(log_gamma[:, None] * (idx + 1.0)[None])[..., None] * np.ones((1, 1, RET_DK))
    k_decay = np.exp(log_gamma[:, None] * (c - 1.0 - idx)[None])[..., None] * np.ones((1, 1, RET_DK))
    chunk_decay = tuple(float(v) for v in np.exp(log_gamma * c))
    return (jnp.asarray(intra * scale, F32), jnp.asarray(q_decay, F32),
            jnp.asarray(k_decay * scale, F32), chunk_decay)


def _hgrn_tables():
    t = np.arange(CHUNK)
    tri = (t[None, :] <= t[:, None]).astype(np.float64)
    m2 = np.concatenate([tri, tri], axis=1)
    diff = t[:, None] ^ t[None, :]
    level = np.where(t[:, None] > t[None, :], np.floor(np.log2(np.maximum(diff, 1))), -1.0)
    return jnp.asarray(m2, BF16), jnp.asarray(level, jnp.int32)


def _prepare_w_in(w_in):
    depth = w_in.shape[0]
    n_qk = 2 * RET_QK
    qk = w_in[:, :, :n_qk].reshape(depth, D_MODEL, 2 * RET_HEADS, RET_DK // 2, 2)
    qk = qk.transpose(0, 1, 2, 4, 3).reshape(depth, D_MODEL, n_qk)
    scale = np.ones((D_IN,), np.float32)
    for off, width in ((OFF_RG, RET_V), (OFF_GQ, HGRN_K), (OFF_GG, HGRN_V), (OFF_MR, D_MODEL), (OFF_MH, D_MODEL)):
        scale[off:off + width] = 0.5
    return qk.astype(BF16), (w_in[:, :, n_qk:] * scale[n_qk:]).astype(BF16)


def _rms(xf, g):
    return xf * lax.rsqrt(jnp.mean(xf * xf, axis=-1, keepdims=True) + EPS) * g


def _norm_inproj_kernel(x_ref, g_ref, wqk_ref, wrest_ref, o_ref):
    h = _rms(x_ref[...], g_ref[...]).astype(BF16)
    n_qk = 2 * RET_QK
    o_ref[:, :n_qk] = jnp.dot(h, wqk_ref[...], preferred_element_type=F32).astype(BF16)
    step = 1024
    for j in range((D_IN - n_qk) // step):
        o_ref[:, n_qk + j * step:n_qk + (j + 1) * step] = jnp.dot(
            h, wrest_ref[:, j * step:(j + 1) * step], preferred_element_type=F32).astype(BF16)


def _norm_inproj(x2, g_all, w_qk_all, w_rest_all, layer):
    rows = x2.shape[0]
    tm = min(ROW_TILE, rows)
    n_qk = 2 * RET_QK
    return pl.pallas_call(
        _norm_inproj_kernel,
        grid=(rows // tm,),
        in_specs=[
            pl.BlockSpec((tm, D_MODEL), lambda i: (i, 0)),
            pl.BlockSpec((None, 1, D_MODEL), lambda i: (layer, 0, 0)),
            pl.BlockSpec((None, D_MODEL, n_qk), lambda i: (layer, 0, 0), pipeline_mode=pl.Buffered(1)),
            pl.BlockSpec((None, D_MODEL, D_IN - n_qk), lambda i: (layer, 0, 0), pipeline_mode=pl.Buffered(1)),
        ],
        out_specs=pl.BlockSpec((tm, D_IN), lambda i: (i, 0)),
        out_shape=jax.ShapeDtypeStruct((rows, D_IN), BF16),
        compiler_params=pltpu.CompilerParams(
            dimension_semantics=("parallel",), vmem_limit_bytes=VMEM_LIMIT),
        name="norm_inproj",
    )(x2, g_all, w_qk_all, w_rest_all)


def _dot_nt(a, b):
    return lax.dot_general(a, b, (((1,), (1,)), ((), ())), preferred_element_type=F32)


def _dot_tn(a, b):
    return lax.dot_general(a, b, (((0,), (0,)), ((), ())), preferred_element_type=F32)


def _silu_half(u):
    return u * (jnp.tanh(u) + 1.0)


def _mixer_kernel(proj_ref, x_ref, cos_ref, sin_ref, intra_ref, qdec_ref, kdec_ref, m2_ref, lvl_ref,
                  gn_ref, hn_ref, lbl_ref, bm_ref, wbr_ref, wbh_ref, wo_ref,
                  o_ref, sret_ref, shg_ref, shn_ref, oret_ref, ohg_ref, qg_ref, kgt_ref, kd_ref, rq_ref, rkt_ref, hx_ref,
                  *, layer, chunk_decay, group):
    @pl.when(pl.program_id(1) == 0)
    def _():
        sret_ref[...] = jnp.zeros_like(sret_ref)
        shg_ref[...] = jnp.zeros_like(shg_ref)
        shn_ref[...] = jnp.zeros_like(shn_ref)

    lbl = lbl_ref[...]
    e = jnp.exp(lbl - jnp.max(lbl, axis=0, keepdims=True))
    sm = e / jnp.sum(e, axis=0, keepdims=True)
    lb = sm[0:1]
    for j in range(1, layer + 1):
        lb = lb + sm[j:j + 1]
    lb = lb - sm[0:1]
    lvl = lvl_ref[...]

    def prepare(sub):
        rows = slice(sub * CHUNK, (sub + 1) * CHUNK)
        slot = sub % 2
        cos = cos_ref[rows]
        sin = sin_ref[rows]
        for h in range(RET_HEADS):
            hs = slice(h * RET_DK, (h + 1) * RET_DK)
            qr = proj_ref[rows, OFF_RQ + h * RET_DK:OFF_RQ + (h + 1) * RET_DK].astype(F32)
            kr = proj_ref[rows, OFF_RK + h * RET_DK:OFF_RK + (h + 1) * RET_DK].astype(F32)
            q = qr * cos + pltpu.roll(qr, RET_DK // 2, 1) * sin
            k = kr * cos + pltpu.roll(kr, RET_DK // 2, 1) * sin
            rq_ref[slot, 0, :, hs] = q.astype(BF16)
            rkt_ref[slot, hs, :] = k.astype(BF16).T
            rq_ref[slot, 1, :, hs] = (q * qdec_ref[h]).astype(BF16)
            rq_ref[slot, 2, :, hs] = (k * kdec_ref[h]).astype(BF16)

        fz = proj_ref[rows, OFF_GF:OFF_GF + HGRN_K].astype(F32)
        sg = 1.0 / (1.0 + jnp.exp(-fz))
        lf = jnp.log(jnp.maximum(lb + (1.0 - lb) * sg, MIN_F))
        hk = (1.0 - lb) * (1.0 - sg)
        gq = proj_ref[rows, OFF_GQ:OFF_GQ + HGRN_K].astype(F32)
        hq = _silu_half(gq)
        lf_hi = lf.astype(BF16)
        lf_lo = (lf - lf_hi.astype(F32)).astype(BF16)
        lf2 = jnp.concatenate([lf_hi, lf_lo], axis=0)

        bs = jnp.dot(m2_ref[...], lf2, preferred_element_type=F32) * LOG2E
        hq16 = hq.astype(BF16)
        hk16 = hk.astype(BF16)
        for lv in range(N_LEVELS):
            half = 1 << lv
            rows_per = max(2 * half, SUBLANES)
            b3 = bs.reshape(CHUNK // rows_per, rows_per, HGRN_K)
            pos = lax.broadcasted_iota(jnp.int32, b3.shape, 1)
            ref = b3[:, half - 1:half, :]
            for blk in range(1, rows_per // (2 * half)):
                ref = jnp.where(pos >= blk * 2 * half,
                                b3[:, blk * 2 * half + half - 1:blk * 2 * half + half, :], ref)
            neg_abs = lax.bitcast_convert_type(
                lax.bitcast_convert_type(b3 - ref, jnp.uint32) | jnp.uint32(0x80000000), F32)
            g = jnp.exp2(neg_abs).reshape(CHUNK, HGRN_K).astype(BF16)
            qg_ref[slot, lv] = hq16 * g
            kgt_ref[slot, lv] = (hk16 * g).T
        qg_ref[slot, N_LEVELS] = (hq * jnp.exp2(bs)).astype(BF16)
        kd_ref[slot] = (hk * jnp.exp2(bs[CHUNK - 1:CHUNK] - bs)).astype(BF16)
        hx_ref[slot, 0:CHUNK] = hq * hk
        hx_ref[slot, CHUNK:CHUNK + SUBLANES] = jnp.broadcast_to(jnp.exp2(bs[CHUNK - 1:CHUNK]), (SUBLANES, HGRN_K))

    def recur(sub):
        rows = slice(sub * CHUNK, (sub + 1) * CHUNK)
        slot = sub % 2
        for h in range(RET_HEADS):
            hs = slice(h * RET_DK, (h + 1) * RET_DK)
            v = proj_ref[rows, OFF_RV + h * RET_DV:OFF_RV + (h + 1) * RET_DV]
            scores = jnp.dot(rq_ref[slot, 0, :, hs], rkt_ref[slot, hs, :], preferred_element_type=F32) * intra_ref[h]
            state = sret_ref[h]
            o = (jnp.dot(scores.astype(BF16), v, preferred_element_type=F32)
                 + jnp.dot(rq_ref[slot, 1, :, hs], state.astype(BF16), preferred_element_type=F32))
            sret_ref[h] = chunk_decay[h] * state + _dot_tn(rq_ref[slot, 2, :, hs], v)
            oret_ref[rows, h * RET_DV:(h + 1) * RET_DV] = (
                o * lax.rsqrt(jnp.mean(o * o, axis=-1, keepdims=True) + EPS))

        for h in range(HGRN_HEADS):
            sl = slice(h * HGRN_DK, (h + 1) * HGRN_DK)
            attn = jnp.zeros((CHUNK, CHUNK), F32)
            for lv in range(N_LEVELS):
                attn = jnp.where(lvl == lv, jnp.dot(qg_ref[slot, lv, :, sl], kgt_ref[slot, lv, sl, :],
                                                    preferred_element_type=F32), attn)
            iv = proj_ref[rows, OFF_GI + h * HGRN_DV:OFF_GI + (h + 1) * HGRN_DV]
            state_t = shg_ref[h]
            diag = jnp.sum(hx_ref[slot, 0:CHUNK, sl], axis=-1, keepdims=True)
            ohg_ref[rows, h * HGRN_DV:(h + 1) * HGRN_DV] = (
                jnp.dot(qg_ref[slot, N_LEVELS, :, sl], shn_ref[h], preferred_element_type=F32)
                + jnp.dot(attn.astype(BF16), iv, preferred_element_type=F32)
                + diag * iv.astype(F32))
            new_state = state_t * hx_ref[slot, CHUNK:CHUNK + 1, sl] + _dot_tn(iv, kd_ref[slot, :, sl])
            shg_ref[h] = new_state
            shn_ref[h] = new_state.astype(BF16).T

    prepare(0)
    for sub in range(group):
        if sub + 1 < group:
            prepare(sub + 1)
        recur(sub)

    rg = proj_ref[:, OFF_RG:OFF_RG + RET_V].astype(F32)
    y_ret = (_silu_half(rg) * (oret_ref[...] * gn_ref[...])).astype(BF16)
    gg = proj_ref[:, OFF_GG:OFF_GG + HGRN_V].astype(F32)
    y_hg = (_rms(ohg_ref[...], hn_ref[...]) * _silu_half(gg)).astype(BF16)
    g_ret = jnp.tanh(proj_ref[:, OFF_MR:OFF_MR + D_MODEL].astype(F32) + bm_ref[0:1]) + 1.0
    g_hg = jnp.tanh(proj_ref[:, OFF_MH:OFF_MH + D_MODEL].astype(F32) + bm_ref[1:2]) + 1.0
    merged = (g_ret * jnp.dot(y_ret, wbr_ref[...], preferred_element_type=F32)
              + g_hg * jnp.dot(y_hg, wbh_ref[...], preferred_element_type=F32))
    o_ref[...] = x_ref[...] + jnp.dot(merged.astype(BF16), wo_ref[...], preferred_element_type=F32)


def _mixer(proj, x2, tables, gn_all, hn_all, lbl, bm_half_all, wbr_all, wbh_all, wo_half_all, *, layer, batch, seq):
    cos, sin, intra, qdec, kdec, chunk_decay, m2, lvl = tables
    n_chunks = seq // CHUNK
    group = int(np.gcd(n_chunks, GROUP))
    n_groups = n_chunks // group
    tile = group * CHUNK
    group_map = lambda b, n: (b * n_groups + n, 0)
    const2 = lambda b, n: (0, 0)
    const3 = lambda b, n: (0, 0, 0)
    this_layer = lambda b, n: (layer, 0, 0)
    depth = lbl.shape[0]
    kern = functools.partial(_mixer_kernel, layer=layer, chunk_decay=chunk_decay, group=group)
    return pl.pallas_call(
        kern,
        grid=(batch, n_groups),
        in_specs=[
            pl.BlockSpec((tile, D_IN), group_map),
            pl.BlockSpec((tile, D_MODEL), group_map),
            pl.BlockSpec((tile, RET_DK), lambda b, n: (n, 0)),
            pl.BlockSpec((tile, RET_DK), lambda b, n: (n, 0)),
            pl.BlockSpec((RET_HEADS, CHUNK, CHUNK), const3),
            pl.BlockSpec((RET_HEADS, CHUNK, RET_DK), const3),
            pl.BlockSpec((RET_HEADS, CHUNK, RET_DK), const3),
            pl.BlockSpec((CHUNK, 2 * CHUNK), const2),
            pl.BlockSpec((CHUNK, CHUNK), const2),
            pl.BlockSpec((None, 1, RET_V), this_layer),
            pl.BlockSpec((None, 1, HGRN_V), this_layer),
            pl.BlockSpec((depth, HGRN_K), const2),
            pl.BlockSpec((None, 2, D_MODEL), this_layer),
            pl.BlockSpec((None, RET_V, D_MODEL), this_layer),
            pl.BlockSpec((None, HGRN_V, D_MODEL), this_layer),
            pl.BlockSpec((None, D_MODEL, D_MODEL), this_layer),
        ],
        out_specs=pl.BlockSpec((tile, D_MODEL), group_map),
        out_shape=jax.ShapeDtypeStruct(x2.shape, F32),
        scratch_shapes=[
            pltpu.VMEM((RET_HEADS, RET_DK, RET_DV), F32),
            pltpu.VMEM((HGRN_HEADS, HGRN_DV, HGRN_DK), F32),
            pltpu.VMEM((HGRN_HEADS, HGRN_DK, HGRN_DV), BF16),
            pltpu.VMEM((tile, RET_V), F32),
            pltpu.VMEM((tile, HGRN_V), F32),
            pltpu.VMEM((2, N_LEVELS + 1, CHUNK, HGRN_K), BF16),
            pltpu.VMEM((2, N_LEVELS, HGRN_K, CHUNK), BF16),
            pltpu.VMEM((2, CHUNK, HGRN_K), BF16),
            pltpu.VMEM((2, 3, CHUNK, RET_QK), BF16),
            pltpu.VMEM((2, RET_QK, CHUNK), BF16),
            pltpu.VMEM((2, CHUNK + SUBLANES, HGRN_K), F32),
        ],
        compiler_params=pltpu.CompilerParams(
            dimension_semantics=("parallel", "arbitrary"), vmem_limit_bytes=VMEM_LIMIT),
        name="mixer",
    )(proj, x2, cos, sin, intra, qdec, kdec, m2, lvl, gn_all, hn_all, lbl, bm_half_all, wbr_all, wbh_all, wo_half_all)


def _ffn_kernel(x_ref, g_ref, wu_ref, wd_ref, fg_ref, o_ref, *, final_norm):
    x = x_ref[...]
    h = _rms(x, g_ref[...]).astype(BF16)
    step = 1024
    acc = x
    for j in range(D_FF // step):
        a = jnp.maximum(jnp.dot(h, wu_ref[:, j * step:(j + 1) * step], preferred_element_type=F32), 0.0)
        acc = acc + jnp.dot((a * a).astype(BF16), wd_ref[j * step:(j + 1) * step, :],
                            preferred_element_type=F32)
    if final_norm:
        acc = _rms(acc, fg_ref[...])
    o_ref[...] = acc


def _ffn(x2, g_all, wu_all, wd_all, fg, *, layer, final_norm):
    rows = x2.shape[0]
    tm = min(FFN_ROW_TILE, rows)
    return pl.pallas_call(
        functools.partial(_ffn_kernel, final_norm=final_norm),
        grid=(rows // tm,),
        in_specs=[
            pl.BlockSpec((tm, D_MODEL), lambda i: (i, 0)),
            pl.BlockSpec((None, 1, D_MODEL), lambda i: (layer, 0, 0)),
            pl.BlockSpec((None, D_MODEL, D_FF), lambda i: (layer, 0, 0), pipeline_mode=pl.Buffered(1)),
            pl.BlockSpec((None, D_FF, D_MODEL), lambda i: (layer, 0, 0), pipeline_mode=pl.Buffered(1)),
            pl.BlockSpec((1, D_MODEL), lambda i: (0, 0)),
        ],
        out_specs=pl.BlockSpec((tm, D_MODEL), lambda i: (i, 0)),
        out_shape=jax.ShapeDtypeStruct(x2.shape, F32),
        compiler_params=pltpu.CompilerParams(
            dimension_semantics=("parallel",), vmem_limit_bytes=VMEM_LIMIT),
        name="ffn",
    )(x2, g_all, wu_all, wd_all, fg)


def kernel(x, norm_mix_g, w_in, ret_gn_g, hgrn_norm_g, hgrn_lb_logits, w_br_ret, w_br_hgrn, b_merge, w_out,
           norm_ffn_g, w_ffn_up, w_ffn_down, final_norm_g):
    batch, seq, d_model = x.shape
    depth = w_in.shape[0]
    assert d_model == D_MODEL and seq % CHUNK == 0 and w_in.shape[2] == D_IN
    assert (batch * seq) % min(ROW_TILE, batch * seq) == 0 and (batch * seq) % min(FFN_ROW_TILE, batch * seq) == 0

    cos, sin = _rotation_tables(seq)
    intra, qdec, kdec, chunk_decay = _retention_tables()
    m2, lvl = _hgrn_tables()
    tables = (cos, sin, intra, qdec, kdec, chunk_decay, m2, lvl)

    w_qk_all, w_rest_all = _prepare_w_in(w_in)
    wbr_all = w_br_ret.astype(BF16)
    wbh_all = w_br_hgrn.astype(BF16)
    wo_half_all = (0.5 * w_out).astype(BF16)
    wu_all = w_ffn_up.astype(BF16)
    wd_all = w_ffn_down.astype(BF16)
    bm_half_all = 0.5 * b_merge
    g_mix = norm_mix_g.reshape(depth, 1, d_model)
    g_ffn = norm_ffn_g.reshape(depth, 1, d_model)
    gn_all = ret_gn_g.reshape(depth, 1, RET_V)
    hn_all = hgrn_norm_g.reshape(depth, 1, HGRN_V)

    x2 = x.reshape(batch * seq, d_model)
    lbl = hgrn_lb_logits.astype(F32)
    fg = final_norm_g.reshape(1, d_model)
    for l in range(depth):
        proj = _norm_inproj(x2, g_mix, w_qk_all, w_rest_all, l)
        x2 = _mixer(proj, x2, tables, gn_all, hn_all, lbl, bm_half_all, wbr_all, wbh_all, wo_half_all,
                    layer=l, batch=batch, seq=seq)
        x2 = _ffn(x2, g_ffn, wu_all, wd_all, fg, layer=l, final_norm=(l == depth - 1))
    return x2.reshape(batch, seq, d_model)
```

```python
import functools

import numpy as np
import jax
import jax.numpy as jnp
from jax import lax
from jax.experimental import pallas as pl
from jax.experimental.pallas import tpu as pltpu

F32 = jnp.float32
BF16 = jnp.bfloat16

D_MODEL = 1024
RET_HEADS = 4
RET_DK = 128
RET_DV = 256
HGRN_HEADS = 4
HGRN_DK = 128
HGRN_DV = 128
ROPE_BASE = 10000.0
D_FF = 4 * D_MODEL
EPS = 1e-6
MIN_F = 1e-30
LOG2E = 1.4426950408889634

RET_QK = RET_HEADS * RET_DK
RET_V = RET_HEADS * RET_DV
HGRN_K = HGRN_HEADS * HGRN_DK
HGRN_V = HGRN_HEADS * HGRN_DV
IN_WIDTHS = (RET_QK, RET_QK, RET_V, RET_V, HGRN_K, HGRN_K, HGRN_V, HGRN_V, D_MODEL, D_MODEL)
D_IN = sum(IN_WIDTHS)
_OFF = tuple(int(v) for v in np.concatenate([[0], np.cumsum(IN_WIDTHS)]))
OFF_RQ, OFF_RK, OFF_RV, OFF_RG, OFF_GQ, OFF_GF, OFF_GI, OFF_GG, OFF_MR, OFF_MH = _OFF[:10]

CHUNK = 128
N_LEVELS = 7
SUBLANES = 8
ROW_TILE = 1024
GROUP = 4
VMEM_LIMIT = 60 * 1024 * 1024


def _rotation_tables(seq):
    angle = 1.0 / (ROPE_BASE ** jnp.linspace(0.0, 1.0, RET_DK // 2, dtype=F32))
    phase = jnp.arange(seq, dtype=F32)[:, None] * angle[None, :]
    cos = jnp.cos(phase)
    sin = jnp.sin(phase)
    return jnp.concatenate([cos, cos], axis=1), jnp.concatenate([-sin, sin], axis=1)


def _retention_tables():
    c = CHUNK
    log_gamma = np.log1p(-np.exp2(-5.0 - np.arange(RET_HEADS, dtype=np.float64)))
    idx = np.arange(c, dtype=np.float64)
    dist = idx[:, None] - idx[None, :]
    causal = dist >= 0
    scale = RET_DK ** -0.5
    intra = np.where(causal[None], np.exp(log_gamma[:, None, None] * np.where(causal, dist, 0.0)[None]), 0.0)
    q_decay = np.exp(log_gamma[:, None] * (idx + 1.0)[None])[..., None] * np.ones((1, 1, RET_DK))
    k_decay = np.exp(log_gamma[:, None] * (c - 1.0 - idx)[None])[..., None] * np.ones((1, 1, RET_DK))
    chunk_decay = tuple(float(v) for v in np.exp(log_gamma * c))
    return (jnp.asarray(intra * scale, F32), jnp.asarray(q_decay, F32),
            jnp.asarray(k_decay * scale, F32), chunk_decay)


def _hgrn_tables():
    t = np.arange(CHUNK)
    tri = (t[None, :] <= t[:, None]).astype(np.float64)
    m2 = np.concatenate([tri, tri], axis=1)
    diff = t[:, None] ^ t[None, :]
    level = np.where(t[:, None] > t[None, :], np.floor(np.log2(np.maximum(diff, 1))), -1.0)
    return jnp.asarray(m2, BF16), jnp.asarray(level, jnp.int32)


def _prepare_w_in(w_in):
    depth = w_in.shape[0]
    n_qk = 2 * RET_QK
    qk = w_in[:, :, :n_qk].reshape(depth, D_MODEL, 2 * RET_HEADS, RET_DK // 2, 2)
    qk = qk.transpose(0, 1, 2, 4, 3).reshape(depth, D_MODEL, n_qk)
    scale = np.ones((D_IN,), np.float32)
    for off, width in ((OFF_RG, RET_V), (OFF_GQ, HGRN_K), (OFF_GG, HGRN_V), (OFF_MR, D_MODEL), (OFF_MH, D_MODEL)):
        scale[off:off + width] = 0.5
    return qk.astype(BF16), (w_in[:, :, n_qk:] * scale[n_qk:]).astype(BF16)


def _rms(xf, g):
    return xf * lax.rsqrt(jnp.mean(xf * xf, axis=-1, keepdims=True) + EPS) * g


def _norm_inproj_kernel(x_ref, g_ref, wqk_ref, wrest_ref, o_ref):
    h = _rms(x_ref[...], g_ref[...]).astype(BF16)
    n_qk = 2 * RET_QK
    o_ref[:, :n_qk] = jnp.dot(h, wqk_ref[...], preferred_element_type=F32).astype(BF16)
    step = 1024
    for j in range((D_IN - n_qk) // step):
        o_ref[:, n_qk + j * step:n_qk + (j + 1) * step] = jnp.dot(
            h, wrest_ref[:, j * step:(j + 1) * step], preferred_element_type=F32).astype(BF16)


def _norm_inproj(x2, g_all, w_qk_all, w_rest_all, layer):
    rows = x2.shape[0]
    tm = min(ROW_TILE, rows)
    n_qk = 2 * RET_QK
    return pl.pallas_call(
        _norm_inproj_kernel,
        grid=(rows // tm,),
        in_specs=[
            pl.BlockSpec((tm, D_MODEL), lambda i: (i, 0)),
            pl.BlockSpec((None, 1, D_MODEL), lambda i: (layer, 0, 0)),
            pl.BlockSpec((None, D_MODEL, n_qk), lambda i: (layer, 0, 0), pipeline_mode=pl.Buffered(1)),
            pl.BlockSpec((None, D_MODEL, D_IN - n_qk), lambda i: (layer, 0, 0), pipeline_mode=pl.Buffered(1)),
        ],
        out_specs=pl.BlockSpec((tm, D_IN), lambda i: (i, 0)),
        out_shape=jax.ShapeDtypeStruct((rows, D_IN), BF16),
        compiler_params=pltpu.CompilerParams(
            dimension_semantics=("parallel",), vmem_limit_bytes=VMEM_LIMIT),
        name="norm_inproj",
    )(x2, g_all, w_qk_all, w_rest_all)


def _dot_nt(a, b):
    return lax.dot_general(a, b, (((1,), (1,)), ((), ())), preferred_element_type=F32)


def _dot_tn(a, b):
    return lax.dot_general(a, b, (((0,), (0,)), ((), ())), preferred_element_type=F32)


def _silu_half(u):
    return u * (jnp.tanh(u) + 1.0)


def _mixer_kernel(proj_ref, x_ref, cos_ref, sin_ref, intra_ref, qdec_ref, kdec_ref, m2_ref, lvl_ref,
                  gn_ref, hn_ref, lbl_ref, bm_ref, wbr_ref, wbh_ref, wo_ref,
                  o_ref, sret_ref, shg_ref, oret_ref, ohg_ref, qg_ref, kgt_ref, kd_ref, rq_ref, rkt_ref, hx_ref,
                  *, layer, chunk_decay, group):
    @pl.when(pl.program_id(1) == 0)
    def _():
        sret_ref[...] = jnp.zeros_like(sret_ref)
        shg_ref[...] = jnp.zeros_like(shg_ref)

    lbl = lbl_ref[...]
    e = jnp.exp(lbl - jnp.max(lbl, axis=0, keepdims=True))
    sm = e / jnp.sum(e, axis=0, keepdims=True)
    lb = sm[0:1]
    for j in range(1, layer + 1):
        lb = lb + sm[j:j + 1]
    lb = lb - sm[0:1]
    lvl = lvl_ref[...]

    def prepare(sub):
        rows = slice(sub * CHUNK, (sub + 1) * CHUNK)
        slot = sub % 2
        cos = cos_ref[rows]
        sin = sin_ref[rows]
        for h in range(RET_HEADS):
            hs = slice(h * RET_DK, (h + 1) * RET_DK)
            qr = proj_ref[rows, OFF_RQ + h * RET_DK:OFF_RQ + (h + 1) * RET_DK].astype(F32)
            kr = proj_ref[rows, OFF_RK + h * RET_DK:OFF_RK + (h + 1) * RET_DK].astype(F32)
            q = qr * cos + pltpu.roll(qr, RET_DK // 2, 1) * sin
            k = kr * cos + pltpu.roll(kr, RET_DK // 2, 1) * sin
            rq_ref[slot, 0, :, hs] = q.astype(BF16)
            rkt_ref[slot, hs, :] = k.astype(BF16).T
            rq_ref[slot, 1, :, hs] = (q * qdec_ref[h]).astype(BF16)
            rq_ref[slot, 2, :, hs] = (k * kdec_ref[h]).astype(BF16)

        fz = proj_ref[rows, OFF_GF:OFF_GF + HGRN_K].astype(F32)
        sg = 1.0 / (1.0 + jnp.exp(-fz))
        lf = jnp.log(jnp.maximum(lb + (1.0 - lb) * sg, MIN_F))
        hk = (1.0 - lb) * (1.0 - sg)
        gq = proj_ref[rows, OFF_GQ:OFF_GQ + HGRN_K].astype(F32)
        hq = _silu_half(gq)
        lf_hi = lf.astype(BF16)
        lf_lo = (lf - lf_hi.astype(F32)).astype(BF16)
        lf2 = jnp.concatenate([lf_hi, lf_lo], axis=0)

        bs = jnp.dot(m2_ref[...], lf2, preferred_element_type=F32) * LOG2E
        hq16 = hq.astype(BF16)
        hk16 = hk.astype(BF16)
        for lv in range(N_LEVELS):
            half = 1 << lv
            rows_per = max(2 * half, SUBLANES)
            b3 = bs.reshape(CHUNK // rows_per, rows_per, HGRN_K)
            pos = lax.broadcasted_iota(jnp.int32, b3.shape, 1)
            ref = b3[:, half - 1:half, :]
            for blk in range(1, rows_per // (2 * half)):
                ref = jnp.where(pos >= blk * 2 * half,
                                b3[:, blk * 2 * half + half - 1:blk * 2 * half + half, :], ref)
            g = jnp.exp2(-jnp.abs(b3 - ref)).reshape(CHUNK, HGRN_K).astype(BF16)
            qg_ref[slot, lv] = hq16 * g
            kgt_ref[slot, lv] = (hk16 * g).T
        qg_ref[slot, N_LEVELS] = (hq * jnp.exp2(bs)).astype(BF16)
        kd_ref[slot] = (hk * jnp.exp2(bs[CHUNK - 1:CHUNK] - bs)).astype(BF16)
        hx_ref[slot, 0:CHUNK] = hq * hk
        hx_ref[slot, CHUNK:CHUNK + SUBLANES] = jnp.broadcast_to(jnp.exp2(bs[CHUNK - 1:CHUNK]), (SUBLANES, HGRN_K))

    def recur(sub):
        rows = slice(sub * CHUNK, (sub + 1) * CHUNK)
        slot = sub % 2
        for h in range(RET_HEADS):
            hs = slice(h * RET_DK, (h + 1) * RET_DK)
            v = proj_ref[rows, OFF_RV + h * RET_DV:OFF_RV + (h + 1) * RET_DV]
            scores = jnp.dot(rq_ref[slot, 0, :, hs], rkt_ref[slot, hs, :], preferred_element_type=F32) * intra_ref[h]
            state = sret_ref[h]
            o = (jnp.dot(scores.astype(BF16), v, preferred_element_type=F32)
                 + jnp.dot(rq_ref[slot, 1, :, hs], state.astype(BF16), preferred_element_type=F32))
            sret_ref[h] = chunk_decay[h] * state + _dot_tn(rq_ref[slot, 2, :, hs], v)
            oret_ref[rows, h * RET_DV:(h + 1) * RET_DV] = (
                o * lax.rsqrt(jnp.mean(o * o, axis=-1, keepdims=True) + EPS))

        for h in range(HGRN_HEADS):
            sl = slice(h * HGRN_DK, (h + 1) * HGRN_DK)
            attn = jnp.zeros((CHUNK, CHUNK), F32)
            for lv in range(N_LEVELS):
                attn = jnp.where(lvl == lv, jnp.dot(qg_ref[slot, lv, :, sl], kgt_ref[slot, lv, sl, :],
                                                    preferred_element_type=F32), attn)
            iv = proj_ref[rows, OFF_GI + h * HGRN_DV:OFF_GI + (h + 1) * HGRN_DV]
            state_t = shg_ref[h]
            diag = jnp.sum(hx_ref[slot, 0:CHUNK, sl], axis=-1, keepdims=True)
            ohg_ref[rows, h * HGRN_DV:(h + 1) * HGRN_DV] = (
                _dot_nt(qg_ref[slot, N_LEVELS, :, sl], state_t.astype(BF16))
                + jnp.dot(attn.astype(BF16), iv, preferred_element_type=F32)
                + diag * iv.astype(F32))
            shg_ref[h] = (state_t * hx_ref[slot, CHUNK:CHUNK + 1, sl]
                          + _dot_tn(iv, kd_ref[slot, :, sl]))

    prepare(0)
    for sub in range(group):
        if sub + 1 < group:
            prepare(sub + 1)
        recur(sub)

    rg = proj_ref[:, OFF_RG:OFF_RG + RET_V].astype(F32)
    y_ret = (_silu_half(rg) * (oret_ref[...] * gn_ref[...])).astype(BF16)
    gg = proj_ref[:, OFF_GG:OFF_GG + HGRN_V].astype(F32)
    y_hg = (_rms(ohg_ref[...], hn_ref[...]) * _silu_half(gg)).astype(BF16)
    g_ret = jnp.tanh(proj_ref[:, OFF_MR:OFF_MR + D_MODEL].astype(F32) + bm_ref[0:1]) + 1.0
    g_hg = jnp.tanh(proj_ref[:, OFF_MH:OFF_MH + D_MODEL].astype(F32) + bm_ref[1:2]) + 1.0
    merged = (g_ret * jnp.dot(y_ret, wbr_ref[...], preferred_element_type=F32)
              + g_hg * jnp.dot(y_hg, wbh_ref[...], preferred_element_type=F32))
    o_ref[...] = x_ref[...] + jnp.dot(merged.astype(BF16), wo_ref[...], preferred_element_type=F32)


def _mixer(proj, x2, tables, gn_all, hn_all, lbl, bm_half_all, wbr_all, wbh_all, wo_half_all, *, layer, batch, seq):
    cos, sin, intra, qdec, kdec, chunk_decay, m2, lvl = tables
    n_chunks = seq // CHUNK
    group = int(np.gcd(n_chunks, GROUP))
    n_groups = n_chunks // group
    tile = group * CHUNK
    group_map = lambda b, n: (b * n_groups + n, 0)
    const2 = lambda b, n: (0, 0)
    const3 = lambda b, n: (0, 0, 0)
    this_layer = lambda b, n: (layer, 0, 0)
    depth = lbl.shape[0]
    kern = functools.partial(_mixer_kernel, layer=layer, chunk_decay=chunk_decay, group=group)
    return pl.pallas_call(
        kern,
        grid=(batch, n_groups),
        in_specs=[
            pl.BlockSpec((tile, D_IN), group_map),
            pl.BlockSpec((tile, D_MODEL), group_map),
            pl.BlockSpec((tile, RET_DK), lambda b, n: (n, 0)),
            pl.BlockSpec((tile, RET_DK), lambda b, n: (n, 0)),
            pl.BlockSpec((RET_HEADS, CHUNK, CHUNK), const3),
            pl.BlockSpec((RET_HEADS, CHUNK, RET_DK), const3),
            pl.BlockSpec((RET_HEADS, CHUNK, RET_DK), const3),
            pl.BlockSpec((CHUNK, 2 * CHUNK), const2),
            pl.BlockSpec((CHUNK, CHUNK), const2),
            pl.BlockSpec((None, 1, RET_V), this_layer),
            pl.BlockSpec((None, 1, HGRN_V), this_layer),
            pl.BlockSpec((depth, HGRN_K), const2),
            pl.BlockSpec((None, 2, D_MODEL), this_layer),
            pl.BlockSpec((None, RET_V, D_MODEL), this_layer),
            pl.BlockSpec((None, HGRN_V, D_MODEL), this_layer),
            pl.BlockSpec((None, D_MODEL, D_MODEL), this_layer),
        ],
        out_specs=pl.BlockSpec((tile, D_MODEL), group_map),
        out_shape=jax.ShapeDtypeStruct(x2.shape, F32),
        scratch_shapes=[
            pltpu.VMEM((RET_HEADS, RET_DK, RET_DV), F32),
            pltpu.VMEM((HGRN_HEADS, HGRN_DV, HGRN_DK), F32),
            pltpu.VMEM((tile, RET_V), F32),
            pltpu.VMEM((tile, HGRN_V), F32),
            pltpu.VMEM((2, N_LEVELS + 1, CHUNK, HGRN_K), BF16),
            pltpu.VMEM((2, N_LEVELS, HGRN_K, CHUNK), BF16),
            pltpu.VMEM((2, CHUNK, HGRN_K), BF16),
            pltpu.VMEM((2, 3, CHUNK, RET_QK), BF16),
            pltpu.VMEM((2, RET_QK, CHUNK), BF16),
            pltpu.VMEM((2, CHUNK + SUBLANES, HGRN_K), F32),
        ],
        compiler_params=pltpu.CompilerParams(
            dimension_semantics=("parallel", "arbitrary"), vmem_limit_bytes=VMEM_LIMIT),
        name="mixer",
    )(proj, x2, cos, sin, intra, qdec, kdec, m2, lvl, gn_all, hn_all, lbl, bm_half_all, wbr_all, wbh_all, wo_half_all)


def _ffn_kernel(x_ref, g_ref, wu_ref, wd_ref, fg_ref, o_ref, *, final_norm):
    x = x_ref[...]
    h = _rms(x, g_ref[...]).astype(BF16)
    step = 1024
    acc = x
    for j in range(D_FF // step):
        a = jnp.maximum(jnp.dot(h, wu_ref[:, j * step:(j + 1) * step], preferred_element_type=F32), 0.0)
        acc = acc + jnp.dot((a * a).astype(BF16), wd_ref[j * step:(j + 1) * step, :],
                            preferred_element_type=F32)
    if final_norm:
        acc = _rms(acc, fg_ref[...])
    o_ref[...] = acc


def _ffn(x2, g_all, wu_all, wd_all, fg, *, layer, final_norm):
    rows = x2.shape[0]
    tm = min(ROW_TILE, rows)
    return pl.pallas_call(
        functools.partial(_ffn_kernel, final_norm=final_norm),
        grid=(rows // tm,),
        in_specs=[
            pl.BlockSpec((tm, D_MODEL), lambda i: (i, 0)),
            pl.BlockSpec((None, 1, D_MODEL), lambda i: (layer, 0, 0)),
            pl.BlockSpec((None, D_MODEL, D_FF), lambda i: (layer, 0, 0), pipeline_mode=pl.Buffered(1)),
            pl.BlockSpec((None, D_FF, D_MODEL), lambda i: (layer, 0, 0), pipeline_mode=pl.Buffered(1)),
            pl.BlockSpec((1, D_MODEL), lambda i: (0, 0)),
        ],
        out_specs=pl.BlockSpec((tm, D_MODEL), lambda i: (i, 0)),
        out_shape=jax.ShapeDtypeStruct(x2.shape, F32),
        compiler_params=pltpu.CompilerParams(
            dimension_semantics=("parallel",), vmem_limit_bytes=VMEM_LIMIT),
        name="ffn",
    )(x2, g_all, wu_all, wd_all, fg)


def kernel(x, norm_mix_g, w_in, ret_gn_g, hgrn_norm_g, hgrn_lb_logits, w_br_ret, w_br_hgrn, b_merge, w_out,
           norm_ffn_g, w_ffn_up, w_ffn_down, final_norm_g):
    batch, seq, d_model = x.shape
    depth = w_in.shape[0]
    assert d_model == D_MODEL and seq % CHUNK == 0 and w_in.shape[2] == D_IN
    assert (batch * seq) % min(ROW_TILE, batch * seq) == 0

    cos, sin = _rotation_tables(seq)
    intra, qdec, kdec, chunk_decay = _retention_tables()
    m2, lvl = _hgrn_tables()
    tables = (cos, sin, intra, qdec, kdec, chunk_decay, m2, lvl)

    w_qk_all, w_rest_all = _prepare_w_in(w_in)
    wbr_all = w_br_ret.astype(BF16)
    wbh_all = w_br_hgrn.astype(BF16)
    wo_half_all = (0.5 * w_out).astype(BF16)
    wu_all = w_ffn_up.astype(BF16)
    wd_all = w_ffn_down.astype(BF16)
    bm_half_all = 0.5 * b_merge
    g_mix = norm_mix_g.reshape(depth, 1, d_model)
    g_ffn = norm_ffn_g.reshape(depth, 1, d_model)
    gn_all = ret_gn_g.reshape(depth, 1, RET_V)
    hn_all = hgrn_norm_g.reshape(depth, 1, HGRN_V)

    x2 = x.reshape(batch * seq, d_model)
    lbl = hgrn_lb_logits.astype(F32)
    fg = final_norm_g.reshape(1, d_model)
    for l in range(depth):
        proj = _norm_inproj(x2, g_mix, w_qk_all, w_rest_all, l)
        x2 = _mixer(proj, x2, tables, gn_all, hn_all, lbl, bm_half_all, wbr_all, wbh_all, wo_half_all,
                    layer=l, batch=batch, seq=seq)
        x2 = _ffn(x2, g_ffn, wu_all, wd_all, fg, layer=l, final_norm=(l == depth - 1))
    return x2.reshape(batch, seq, d_model)
```
